```python
import math
import jax, jax.numpy as jnp
from jax import lax
import numpy as np

D_MODEL = 4096
BATCH = 4
SEQ = 4096
DEPTH = 2

GRID_W = 64
CTX_LEN = 256
N_MIXERS = 2
EPS = 1e-6

DA_HEADS = 16
DA_HEAD_DIM = 128
DA_V_DIM = 2 * DA_HEAD_DIM
DA_Q_W = DA_HEADS * 2 * DA_HEAD_DIM
DA_K_W = DA_HEADS * 2 * DA_HEAD_DIM
DA_V_W = DA_HEADS * DA_V_DIM
DA_GATE_W = DA_V_W
DA_IN_W = DA_Q_W + DA_K_W + DA_V_W + DA_GATE_W
DA_K0 = DA_Q_W
DA_V0 = DA_Q_W + DA_K_W
DA_G0 = DA_Q_W + DA_K_W + DA_V_W
Q_BLOCK = 128
ROPE_BASE = 10000.0

S5_WIDTH = D_MODEL
S5_GROUP = 16
S5_GROUPS = S5_WIDTH // S5_GROUP
S5_STATE = 64
DT_MIN = 1e-3
DT_MAX = 1e-1

N_ATTN_LAYERS = (DEPTH + 1) // 2
N_S5_LAYERS = DEPTH // 2

kernel_name = 'hybrid_diffattn_s5_prefix_dit'


def rmsnorm(x, g):
    xf = x.astype(jnp.float32)
    y = xf * lax.rsqrt(jnp.mean(xf * xf, axis=-1, keepdims=True) + EPS)
    return (y * g.astype(jnp.float32)).astype(x.dtype)


def axial_rope_tables(n_tokens):
    rows = n_tokens // GRID_W
    row = jnp.repeat(jnp.arange(rows), GRID_W)
    col = jnp.tile(jnp.arange(GRID_W), rows)
    n_freq = DA_HEAD_DIM // 4
    inv_freq = ROPE_BASE ** (-jnp.arange(n_freq, dtype=jnp.float32) / n_freq)
    ang = jnp.stack([row, col], axis=-1).astype(jnp.float32)[:, :, None] * inv_freq
    return jnp.cos(ang)[:, :, None, :], jnp.sin(ang)[:, :, None, :]


def apply_axial_rope(t, cos, sin):
    shp = t.shape
    tr = t.reshape(shp[:-1] + (2, 2, shp[-1] // 4))
    rot = jnp.stack([-tr[..., 1, :], tr[..., 0, :]], axis=-2)
    return (tr * cos.astype(t.dtype) + rot * sin.astype(t.dtype)).reshape(shp)


def split_qk(t):
    b, n = t.shape[:2]
    return t.reshape(b, n, DA_HEADS, 2, DA_HEAD_DIM).transpose(3, 0, 2, 1, 4)


def split_v(t):
    b, n = t.shape[:2]
    return t.reshape(b, n, DA_HEADS, DA_V_DIM).transpose(0, 2, 1, 3)


def diff_softmax_core(q, k, v, lam):
    s = jnp.einsum('nbhqd,nbhkd->nbhqk', q, k).astype(jnp.float32) * (DA_HEAD_DIM ** -0.5)
    p = jax.nn.softmax(s, axis=-1)
    w = p[0] - lam * p[1]
    return jnp.einsum('bhqk,bhkv->bhqv', w.astype(v.dtype), v)


def diff_attn_mixer(h_lat, h_ctx, w_in, w_out, lam_vecs, subln_g, lam_init, need_ctx_out):
    bsz, n_lat, _ = h_lat.shape
    p_lat = h_lat @ w_in
    cos, sin = axial_rope_tables(n_lat)
    q_lat = apply_axial_rope(split_qk(p_lat[..., :DA_K0]), cos, sin)
    k_lat = apply_axial_rope(split_qk(p_lat[..., DA_K0:DA_V0]), cos, sin)
    v_lat = split_v(p_lat[..., DA_V0:DA_G0])
    g_lat = p_lat[..., DA_G0:]
    if need_ctx_out:
        p_ctx = h_ctx @ w_in
        kv_ctx = p_ctx[..., DA_K0:DA_G0]
    else:
        kv_ctx = h_ctx @ w_in[:, DA_K0:DA_G0]
    k_ctx = split_qk(kv_ctx[..., :DA_K_W])
    v_ctx = split_v(kv_ctx[..., DA_K_W:])
    lv = lam_vecs.astype(jnp.float32)
    lam = jnp.exp(jnp.sum(lv[0] * lv[1])) - jnp.exp(jnp.sum(lv[2] * lv[3])) + lam_init
    k_all = jnp.concatenate([k_ctx, k_lat], axis=3)
    v_all = jnp.concatenate([v_ctx, v_lat], axis=2)
    n_blk = n_lat // Q_BLOCK
    q_blocks = q_lat.reshape(2, bsz, DA_HEADS, n_blk, Q_BLOCK, DA_HEAD_DIM).transpose(3, 0, 1, 2, 4, 5)
    o_blocks = lax.map(lambda qb: diff_softmax_core(qb, k_all, v_all, lam), q_blocks)
    o_lat = o_blocks.transpose(1, 2, 0, 3, 4).reshape(bsz, DA_HEADS, n_lat, DA_V_DIM)

    def finish(o, gate):
        o = rmsnorm(o, subln_g) * (1.0 - lam_init)
        o = o.transpose(0, 2, 1, 3).reshape(o.shape[0], o.shape[2], DA_V_W)
        return (o * jax.nn.silu(gate)) @ w_out

    out_lat = finish(o_lat, g_lat)
    out_ctx = None
    if need_ctx_out:
        q_ctx = split_qk(p_ctx[..., :DA_K0])
        o_ctx = diff_softmax_core(q_ctx, k_ctx, v_ctx, lam)
        out_ctx = finish(o_ctx, p_ctx[..., DA_G0:])
    return out_lat, out_ctx


def s5_discretize(A_re, A_im, log_dt, B_re, B_im):
    A_re = A_re.astype(jnp.float32)
    A_im = A_im.astype(jnp.float32)
    dt = jnp.exp(log_dt.astype(jnp.float32))[:, None]
    mag = jnp.exp(A_re * dt)
    a_re = mag * jnp.cos(A_im * dt)
    a_im = mag * jnp.sin(A_im * dt)
    den = A_re * A_re + A_im * A_im
    f_re = ((a_re - 1.0) * A_re + a_im * A_im) / den
    f_im = (a_im * A_re - (a_re - 1.0) * A_im) / den
    B_re = B_re.astype(jnp.float32)
    B_im = B_im.astype(jnp.float32)
    bb_re = f_re[..., None] * B_re - f_im[..., None] * B_im
    bb_im = f_re[..., None] * B_im + f_im[..., None] * B_re
    return a_re, a_im, bb_re, bb_im


def ssm_combine(e1, e2):
    a1r, a1i, b1r, b1i = e1
    a2r, a2i, b2r, b2i = e2
    return (a2r * a1r - a2i * a1i,
            a2r * a1i + a2i * a1r,
            a2r * b1r - a2i * b1i + b2r,
            a2r * b1i + a2i * b1r + b2i)


def s5_scan(u, a_re, a_im, bb_re, bb_im, h0):
    bu_re = jnp.einsum('lbgc,gpc->lbgp', u, bb_re)
    bu_im = jnp.einsum('lbgc,gpc->lbgp', u, bb_im)
    if h0 is not None:
        h0_re, h0_im = h0
        bu_re = bu_re.at[0].add(a_re * h0_re - a_im * h0_im)
        bu_im = bu_im.at[0].add(a_re * h0_im + a_im * h0_re)
    n = u.shape[0]
    ar = jnp.broadcast_to(a_re, (n, 1) + a_re.shape)
    ai = jnp.broadcast_to(a_im, (n, 1) + a_im.shape)
    _, _, h_re, h_im = lax.associative_scan(ssm_combine, (ar, ai, bu_re, bu_im), axis=0)
    return h_re, h_im


def s5_readout(h_re, h_im, C_re, C_im):
    return (jnp.einsum('lbgp,gcp->lbgc', h_re, C_re.astype(jnp.float32))
            - jnp.einsum('lbgp,gcp->lbgc', h_im, C_im.astype(jnp.float32)))


def to_groups(u):
    b, n = u.shape[:2]
    return u.astype(jnp.float32).reshape(b, n, S5_GROUPS, S5_GROUP).transpose(1, 0, 2, 3)


def from_groups(y):
    n, b = y.shape[:2]
    return y.transpose(1, 0, 2, 3).reshape(b, n, S5_WIDTH)


def s5_mixer(h_lat, h_ctx, w_in, A_re, A_im, log_dt, B_re, B_im, C_re, C_im, d_skip, w_glu, w_out, need_ctx_out):
    p_lat = h_lat @ w_in
    u_lat, z_lat = p_lat[..., :S5_WIDTH], p_lat[..., S5_WIDTH:]
    if need_ctx_out:
        p_ctx = h_ctx @ w_in
        u_ctx, z_ctx = p_ctx[..., :S5_WIDTH], p_ctx[..., S5_WIDTH:]
    else:
        u_ctx = h_ctx @ w_in[:, :S5_WIDTH]
    ug_lat = to_groups(u_lat)
    ug_ctx = to_groups(u_ctx)
    y_lat = jnp.zeros_like(ug_lat)
    y_ctx = jnp.zeros_like(ug_ctx)
    for d in range(2):
        a_re, a_im, bb_re, bb_im = s5_discretize(A_re[d], A_im[d], log_dt[d], B_re[d], B_im[d])
        uc = ug_ctx if d == 0 else ug_ctx[::-1]
        ul = ug_lat if d == 0 else ug_lat[::-1]
        hc_re, hc_im = s5_scan(uc, a_re, a_im, bb_re, bb_im, None)
        hl_re, hl_im = s5_scan(ul, a_re, a_im, bb_re, bb_im, (hc_re[-1], hc_im[-1]))
        yl = s5_readout(hl_re, hl_im, C_re[d], C_im[d])
        y_lat = y_lat + (yl if d == 0 else yl[::-1])
        if need_ctx_out:
            yc = s5_readout(hc_re, hc_im, C_re[d], C_im[d])
            y_ctx = y_ctx + (yc if d == 0 else yc[::-1])
    dk = d_skip.astype(jnp.float32)

    def finish(y, u, z):
        y = (from_groups(y) + dk * u.astype(jnp.float32)).astype(u.dtype)
        y = jax.nn.gelu(y)
        y = y * jax.nn.sigmoid(y @ w_glu)
        return (y * jax.nn.silu(z)) @ w_out

    out_lat = finish(y_lat, u_lat, z_lat)
    out_ctx = finish(y_ctx, u_ctx, z_ctx) if need_ctx_out else None
    return out_lat, out_ctx


def setup_inputs(seed: int = 0) -> dict:
    key = jax.random.key(seed)
    ks = jax.random.split(key, 24)
    nrm = jax.random.normal
    E, G, P, C16 = S5_WIDTH, S5_GROUPS, S5_STATE, S5_GROUP
    NA, NS = N_ATTN_LAYERS, N_S5_LAYERS
    x = nrm(ks[0], (BATCH, SEQ, D_MODEL), jnp.float32)
    c = nrm(ks[1], (BATCH, D_MODEL), jnp.float32)
    ctx = nrm(ks[2], (BATCH, CTX_LEN, D_MODEL), jnp.float32)
    c_ctx = nrm(ks[3], (D_MODEL,), jnp.float32)
    ada_w = nrm(ks[4], (DEPTH, D_MODEL, 3 * D_MODEL), jnp.float32) * D_MODEL ** -0.5
    ada_b = 0.01 * nrm(ks[5], (DEPTH, 3 * D_MODEL), jnp.float32)
    norm_pre = 1.0 + 0.02 * nrm(ks[6], (DEPTH, D_MODEL), jnp.float32)
    norm_post = 1.0 + 0.02 * nrm(ks[7], (DEPTH, D_MODEL), jnp.float32)
    attn_w_in = nrm(ks[8], (NA, D_MODEL, DA_IN_W), jnp.float32) * D_MODEL ** -0.5
    attn_w_out = nrm(ks[9], (NA, DA_V_W, D_MODEL), jnp.float32) * DA_V_W ** -0.5
    attn_lam = 0.1 * nrm(ks[10], (NA, 4, DA_HEAD_DIM), jnp.float32)
    attn_subln = 1.0 + 0.02 * nrm(ks[11], (NA, DA_V_DIM), jnp.float32)
    s5_w_in = nrm(ks[12], (NS, D_MODEL, 2 * E), jnp.float32) * D_MODEL ** -0.5
    s5_A_re = -0.5 + 0.01 * nrm(ks[13], (NS, 2, G, P), jnp.float32)
    s5_A_im = math.pi * jnp.arange(P, dtype=jnp.float32) + 0.01 * nrm(ks[14], (NS, 2, G, P), jnp.float32)
    s5_log_dt = jax.random.uniform(ks[15], (NS, 2, G), jnp.float32, math.log(DT_MIN), math.log(DT_MAX))
    s5_B_re = nrm(ks[16], (NS, 2, G, P, C16), jnp.float32) * (2 * C16) ** -0.5
    s5_B_im = nrm(ks[17], (NS, 2, G, P, C16), jnp.float32) * (2 * C16) ** -0.5
    s5_C_re = nrm(ks[18], (NS, 2, G, C16, P), jnp.float32) * (2 * P) ** -0.5
    s5_C_im = nrm(ks[19], (NS, 2, G, C16, P), jnp.float32) * (2 * P) ** -0.5
    s5_D = 0.5 * nrm(ks[20], (NS, E), jnp.float32)
    s5_w_glu = nrm(ks[21], (NS, E, E), jnp.float32) * E ** -0.5
    s5_w_out = nrm(ks[22], (NS, E, D_MODEL), jnp.float32) * E ** -0.5
    return {'x': x, 'c': c, 'ctx': ctx, 'c_ctx': c_ctx,
            'ada_w': ada_w, 'ada_b': ada_b, 'norm_pre': norm_pre, 'norm_post': norm_post,
            'attn_w_in': attn_w_in, 'attn_w_out': attn_w_out, 'attn_lam': attn_lam, 'attn_subln': attn_subln,
            's5_w_in': s5_w_in, 's5_A_re': s5_A_re, 's5_A_im': s5_A_im, 's5_log_dt': s5_log_dt,
            's5_B_re': s5_B_re, 's5_B_im': s5_B_im, 's5_C_re': s5_C_re, 's5_C_im': s5_C_im,
            's5_D': s5_D, 's5_w_glu': s5_w_glu, 's5_w_out': s5_w_out}


def reference(x, c, ctx, c_ctx, ada_w, ada_b, norm_pre, norm_post,
              attn_w_in, attn_w_out, attn_lam, attn_subln,
              s5_w_in, s5_A_re, s5_A_im, s5_log_dt, s5_B_re, s5_B_im, s5_C_re, s5_C_im,
              s5_D, s5_w_glu, s5_w_out):
    x_lat, x_ctx = x, ctx
    for i in range(DEPTH):
        need_ctx_out = i < DEPTH - 1
        mod = jax.nn.silu(c) @ ada_w[i] + ada_b[i]
        mod_c = jax.nn.silu(c_ctx) @ ada_w[i] + ada_b[i]
        shift, scale, gate = jnp.split(mod, 3, axis=-1)
        shift_c, scale_c, gate_c = jnp.split(mod_c, 3, axis=-1)
        h_lat = rmsnorm(x_lat, norm_pre[i]) * (1.0 + scale[:, None, :]) + shift[:, None, :]
        h_ctx = rmsnorm(x_ctx, norm_pre[i]) * (1.0 + scale_c) + shift_c
        j = i // N_MIXERS
        if i % N_MIXERS == 0:
            lam_init = 0.8 - 0.6 * math.exp(-0.3 * i)
            out_lat, out_ctx = diff_attn_mixer(h_lat, h_ctx, attn_w_in[j], attn_w_out[j], attn_lam[j],
                                               attn_subln[j], lam_init, need_ctx_out)
        else:
            out_lat, out_ctx = s5_mixer(h_lat, h_ctx, s5_w_in[j], s5_A_re[j], s5_A_im[j], s5_log_dt[j],
                                        s5_B_re[j], s5_B_im[j], s5_C_re[j], s5_C_im[j], s5_D[j],
                                        s5_w_glu[j], s5_w_out[j], need_ctx_out)
        x_lat = x_lat + gate[:, None, :] * rmsnorm(out_lat, norm_post[i])
        if need_ctx_out:
            x_ctx = x_ctx + gate_c * rmsnorm(out_ctx, norm_post[i])
    return x_lat
```

```python
import functools
import math

import jax
import jax.numpy as jnp
from jax import lax
from jax.experimental import pallas as pl
from jax.experimental.pallas import tpu as pltpu

F32 = jnp.float32
BF16 = jnp.bfloat16

V7X_LANES = 128
V7X_SUBLANES = 8
V7X_VMEM_BYTES = 64 * 1024 * 1024
VMEM_LIMIT = V7X_VMEM_BYTES - 8 * 1024 * 1024

D_MODEL = 4096
GRID_W = 64
EPS = 1e-6
HEADS = 16
HEAD_DIM = 128
V_DIM = 2 * HEAD_DIM
QK_W = HEADS * 2 * HEAD_DIM
ROPE_BASE = 10000.0
S5_GROUP = 16
S5_GROUPS = D_MODEL // S5_GROUP
S5_STATE = 64
CHUNK = 16
CHUNK_W = CHUNK * S5_GROUP
MOD_ROWS = 8


def _params(sem, vmem=VMEM_LIMIT):
    return pltpu.CompilerParams(dimension_semantics=sem, vmem_limit_bytes=vmem)


def _sigmoid(x):
    return 1.0 / (1.0 + jnp.exp(-x))


def _silu(x):
    return x * _sigmoid(x)


def _mod_kernel(c_ref, w_ref, b_ref, o_ref):
    a = _silu(c_ref[...])
    acc = jnp.dot(a, w_ref[0], preferred_element_type=F32, precision=lax.Precision.HIGHEST)
    o_ref[0] = acc + b_ref[0]


def _modulation(cc, ada_w, ada_b):
    depth, d, n = ada_w.shape
    bn = 512
    return pl.pallas_call(
        _mod_kernel,
        grid=(depth, n // bn),
        in_specs=[
            pl.BlockSpec((MOD_ROWS, d), lambda i, j: (0, 0)),
            pl.BlockSpec((1, d, bn), lambda i, j: (i, 0, j)),
            pl.BlockSpec((1, 1, bn), lambda i, j: (i, 0, j)),
        ],
        out_specs=pl.BlockSpec((1, MOD_ROWS, bn), lambda i, j: (i, 0, j)),
        out_shape=jax.ShapeDtypeStruct((depth, MOD_ROWS, n), F32),
        compiler_params=_params(("arbitrary", "arbitrary")),
        name="modulation",
    )(cc, ada_w, ada_b.reshape(depth, 1, n))


def _rms(x):
    return x * lax.rsqrt(jnp.mean(x * x, axis=-1, keepdims=True) + EPS)


def _normmod_kernel(x_ref, g_ref, scale_ref, shift_ref, o_ref):
    y = _rms(x_ref[...]) * g_ref[...]
    o_ref[...] = (y * (1.0 + scale_ref[0]) + shift_ref[0]).astype(o_ref.dtype)


def _resid_kernel(x_ref, o_in_ref, g_ref, gate_ref, o_ref):
    y = _rms(o_in_ref[...].astype(F32)) * g_ref[...]
    o_ref[...] = x_ref[...] + gate_ref[0] * y


def _mod_row_map(rows_per_mod_row, fixed_row):
    if fixed_row is not None:
        return lambda i: (fixed_row, 0, 0)
    return lambda i: (i // rows_per_mod_row, 0, 0)


def _normmod(x2, g, scale3, shift3, *, tiles_per_batch=None, fixed_row=None, bm=256):
    m, d = x2.shape
    row_map = _mod_row_map(tiles_per_batch, fixed_row)
    return pl.pallas_call(
        _normmod_kernel,
        grid=(m // bm,),
        in_specs=[
            pl.BlockSpec((bm, d), lambda i: (i, 0)),
            pl.BlockSpec((1, d), lambda i: (0, 0)),
            pl.BlockSpec((1, 1, d), row_map),
            pl.BlockSpec((1, 1, d), row_map),
        ],
        out_specs=pl.BlockSpec((bm, d), lambda i: (i, 0)),
        out_shape=jax.ShapeDtypeStruct((m, d), BF16),
        compiler_params=_params(("arbitrary",)),
        name="normmod",
    )(x2, g.reshape(1, d), scale3, shift3)


def _resid(x2, o2, g, gate3, *, tiles_per_batch=None, fixed_row=None, bm=256):
    m, d = x2.shape
    row_map = _mod_row_map(tiles_per_batch, fixed_row)
    return pl.pallas_call(
        _resid_kernel,
        grid=(m // bm,),
        in_specs=[
            pl.BlockSpec((bm, d), lambda i: (i, 0)),
            pl.BlockSpec((bm, d), lambda i: (i, 0)),
            pl.BlockSpec((1, d), lambda i: (0, 0)),
            pl.BlockSpec((1, 1, d), row_map),
        ],
        out_specs=pl.BlockSpec((bm, d), lambda i: (i, 0)),
        out_shape=jax.ShapeDtypeStruct((m, d), F32),
        compiler_params=_params(("arbitrary",)),
        name="resid",
    )(x2, o2, g.reshape(1, d), gate3)


def _mm_kernel(x_ref, w_ref, o_ref):
    acc = jnp.dot(x_ref[...], w_ref[...], preferred_element_type=F32)
    o_ref[...] = acc.astype(o_ref.dtype)


def _mm_qk_kernel(x_ref, w_ref, cos_ref, sin_ref, o_ref, *, rope, q_tiles, q_scale):
    acc = jnp.dot(x_ref[...], w_ref[...], preferred_element_type=F32)
    sc = jnp.where(pl.program_id(1) < q_tiles, q_scale, 1.0).astype(F32)
    bn = acc.shape[1]
    if rope:
        cos = cos_ref[...]
        sin = sin_ref[...]
        lane = lax.broadcasted_iota(jnp.int32, (1, HEAD_DIM), 1)
        first = (lane % (HEAD_DIM // 2)) < (HEAD_DIM // 4)
    for c in range(bn // HEAD_DIM):
        xc = acc[:, c * HEAD_DIM:(c + 1) * HEAD_DIM]
        if rope:
            partner = jnp.where(first,
                                pltpu.roll(xc, HEAD_DIM - HEAD_DIM // 4, 1),
                                pltpu.roll(xc, HEAD_DIM // 4, 1))
            xc = xc * cos + partner * sin
        o_ref[:, c * HEAD_DIM:(c + 1) * HEAD_DIM] = (xc * sc).astype(o_ref.dtype)


def _mm_glu_kernel(x_ref, w_ref, y_ref, z_ref, o_ref):
    acc = jnp.dot(x_ref[...], w_ref[...], preferred_element_type=F32)
    y = y_ref[...].astype(F32)
    o_ref[...] = (y * _sigmoid(acc) * _silu(z_ref[...].astype(F32))).astype(o_ref.dtype)


def _mm_tiles(m, n):
    return min(m, 1024), min(n, 1024)


def _matmul(x, w, out_dtype=BF16):
    m, k = x.shape
    n = w.shape[1]
    bm, bn = _mm_tiles(m, n)
    return pl.pallas_call(
        _mm_kernel,
        grid=(m // bm, n // bn),
        in_specs=[pl.BlockSpec((bm, k), lambda i, j: (i, 0)),
                  pl.BlockSpec((k, bn), lambda i, j: (0, j))],
        out_specs=pl.BlockSpec((bm, bn), lambda i, j: (i, j)),
        out_shape=jax.ShapeDtypeStruct((m, n), out_dtype),
        compiler_params=_params(("arbitrary", "arbitrary")),
        name="proj",
    )(x, w)


def _matmul_qk(x, w, cos_t, sin_t, *, rope):
    m, k = x.shape
    n = w.shape[1]
    bm, bn = _mm_tiles(m, n)
    tiles_per_seq = cos_t.shape[0] // bm if rope else 1
    tab_map = (lambda i, j: (i % tiles_per_seq, 0)) if rope else (lambda i, j: (0, 0))
    tab_rows = bm if rope else V7X_SUBLANES
    kern = functools.partial(_mm_qk_kernel, rope=rope, q_tiles=(n // 2) // bn,
                             q_scale=HEAD_DIM ** -0.5)
    return pl.pallas_call(
        kern,
        grid=(m // bm, n // bn),
        in_specs=[pl.BlockSpec((bm, k), lambda i, j: (i, 0)),
                  pl.BlockSpec((k, bn), lambda i, j: (0, j)),
                  pl.BlockSpec((tab_rows, HEAD_DIM), tab_map),
                  pl.BlockSpec((tab_rows, HEAD_DIM), tab_map)],
        out_specs=pl.BlockSpec((bm, bn), lambda i, j: (i, j)),
        out_shape=jax.ShapeDtypeStruct((m, n), BF16),
        compiler_params=_params(("arbitrary", "arbitrary")),
        name="proj_qk_rope" if rope else "proj_qk",
    )(x, w, cos_t, sin_t)


def _matmul_glu(y, w, p):
    m, k = y.shape
    n = w.shape[1]
    bm, bn = _mm_tiles(m, n)
    z_off = n // bn
    return pl.pallas_call(
        _mm_glu_kernel,
        grid=(m // bm, n // bn),
        in_specs=[pl.BlockSpec((bm, k), lambda i, j: (i, 0)),
                  pl.BlockSpec((k, bn), lambda i, j: (0, j)),
                  pl.BlockSpec((bm, bn), lambda i, j: (i, j)),
                  pl.BlockSpec((bm, bn), lambda i, j: (i, z_off + j))],
        out_specs=pl.BlockSpec((bm, bn), lambda i, j: (i, j)),
        out_shape=jax.ShapeDtypeStruct((m, n), BF16),
        compiler_params=_params(("arbitrary", "arbitrary")),
        name="proj_glu",
    )(y, w, y, p)


def _attn_kernel(lam_ref, g_ref, q_ref, gate_ref, *refs, n_seg, lam_init):
    o_ref = refs[2 * n_seg]
    segs = [(refs[2 * i], refs[2 * i + 1]) for i in range(n_seg)]
    lv = lam_ref[...]
    t1 = jnp.sum(lv[0:1] * lv[1:2], axis=1, keepdims=True)
    t2 = jnp.sum(lv[2:3] * lv[3:4], axis=1, keepdims=True)
    lam = jnp.exp(t1) - jnp.exp(t2) + lam_init

    q1 = q_ref[:, :HEAD_DIM]
    q2 = q_ref[:, HEAD_DIM:]
    dn = (((1,), (1,)), ((), ()))
    s1 = [lax.dot_general(q1, k[:, :HEAD_DIM], dn, preferred_element_type=F32) for k, _ in segs]
    s2 = [lax.dot_general(q2, k[:, HEAD_DIM:], dn, preferred_element_type=F32) for k, _ in segs]

    def softmax_parts(ss):
        m = functools.reduce(jnp.maximum, [jnp.max(s, axis=1, keepdims=True) for s in ss])
        ps = [jnp.exp(s - m) for s in ss]
        l = functools.reduce(jnp.add, [jnp.sum(p, axis=1, keepdims=True) for p in ps])
        return ps, l

    p1, l1 = softmax_parts(s1)
    p2, l2 = softmax_parts(s2)
    c1 = 1.0 / l1
    c2 = lam / l2
    o = None
    for a, b, (_, v) in zip(p1, p2, segs):
        w = (a * c1 - b * c2).astype(BF16)
        part = jnp.dot(w, v[...], preferred_element_type=F32)
        o = part if o is None else o + part
    o = _rms(o) * g_ref[...] * (1.0 - lam_init)
    o_ref[...] = (o * _silu(gate_ref[...].astype(F32))).astype(o_ref.dtype)


def _attention(lam_vecs, subln_g, q_arr, gate_arr, segs, *, batch, n_q, tq, lam_init):
    nq_tiles = n_q // tq
    n_seg = len(segs)
    in_specs = [
        pl.BlockSpec((4, HEAD_DIM), lambda b, h, i: (0, 0)),
        pl.BlockSpec((1, V_DIM), lambda b, h, i: (0, 0)),
        pl.BlockSpec((tq, V_DIM), lambda b, h, i: (b * nq_tiles + i, h)),
        pl.BlockSpec((tq, V_DIM), lambda b, h, i: (b * nq_tiles + i, HEADS + h)),
    ]
    args = [lam_vecs, subln_g.reshape(1, V_DIM), q_arr, gate_arr]
    for k_arr, v_arr, rows in segs:
        in_specs.append(pl.BlockSpec((rows, V_DIM), lambda b, h, i: (b, HEADS + h)))
        in_specs.append(pl.BlockSpec((rows, V_DIM), lambda b, h, i: (b, h)))
        args += [k_arr, v_arr]
    kern = functools.partial(_attn_kernel, n_seg=n_seg, lam_init=lam_init)
    return pl.pallas_call(
        kern,
        grid=(batch, HEADS, nq_tiles),
        in_specs=in_specs,
        out_specs=pl.BlockSpec((tq, V_DIM), lambda b, h, i: (b * nq_tiles + i, h)),
        out_shape=jax.ShapeDtypeStruct((batch * n_q, HEADS * V_DIM), BF16),
        compiler_params=_params(("arbitrary", "arbitrary", "arbitrary")),
        name="diff_attn",
    )(*args)


def _s5_scan_kernel(u_ref, w1_ref, w2_ref, r_ref, coef_ref, y_ref, s_scr, *, n_chunks, n_ctx_chunks):
    gb, rows, _ = u_ref.shape
    step_rows = V7X_SUBLANES
    skip = n_ctx_chunks * step_rows
    fwd = (lax.broadcasted_iota(jnp.int32, (rows, 1), 0) & 4) == 0
    fwd_out = (lax.broadcasted_iota(jnp.int32, (rows - skip, 1), 0) & 4) == 0
    half = 2 * S5_STATE

    for gi in range(gb):
        o1 = jnp.dot(u_ref[gi], w1_ref[gi], preferred_element_type=F32)
        s_scr[gi] = jnp.where(fwd, o1[:, :2 * half], o1[:, 2 * half:])

    ca = [coef_ref[gi, 0] for gi in range(gb)]
    cb = [coef_ref[gi, 1] for gi in range(gb)]

    def step(j, hs):
        r0 = pl.multiple_of(j * step_rows, step_rows)
        out = []
        for gi in range(gb):
            h2 = hs[gi]
            s = s_scr[gi, pl.ds(r0, step_rows), :]
            s_scr[gi, pl.ds(r0, step_rows), :] = h2
            sw = jnp.concatenate([h2[:, half:], h2[:, :half]], axis=1)
            out.append(ca[gi] * h2 + cb[gi] * sw + s)
        return tuple(out)

    h0 = tuple(jnp.zeros((step_rows, 2 * half), F32) for _ in range(gb))
    lax.fori_loop(0, n_chunks, step, h0)

    for gi in range(gb):
        hin = s_scr[gi, skip:, :half].astype(BF16)
        o2 = jnp.dot(u_ref[gi, skip:, :], w2_ref[gi], preferred_element_type=F32)
        o2 = o2 + jnp.dot(hin, r_ref[gi], preferred_element_type=F32)
        y_ref[gi] = jnp.where(fwd_out, o2[:, :CHUNK_W], o2[:, CHUNK_W:])


def _s5_scan(useq, w1, w2, r, coef, *, n_chunks, n_ctx_chunks, gb=4):
    g, rows, cw = useq.shape
    out_rows = rows - n_ctx_chunks * V7X_SUBLANES
    kern = functools.partial(_s5_scan_kernel, n_chunks=n_chunks, n_ctx_chunks=n_ctx_chunks)
    return pl.pallas_call(
        kern,
        grid=(g // gb,),
        in_specs=[
            pl.BlockSpec((gb, rows, cw), lambda i: (i, 0, 0)),
            pl.BlockSpec((gb,) + w1.shape[1:], lambda i: (i, 0, 0)),
            pl.BlockSpec((gb,) + w2.shape[1:], lambda i: (i, 0, 0)),
            pl.BlockSpec((gb,) + r.shape[1:], lambda i: (i, 0, 0)),
            pl.BlockSpec((gb,) + coef.shape[1:], lambda i: (i, 0, 0, 0)),
        ],
        out_specs=pl.BlockSpec((gb, out_rows, cw), lambda i: (i, 0, 0)),
        out_shape=jax.ShapeDtypeStruct((g, out_rows, cw), F32),
        scratch_shapes=[pltpu.VMEM((gb, rows, 4 * S5_STATE), F32)],
        compiler_params=_params(("arbitrary",)),
        name="s5_scan",
    )(useq, w1, w2, r, coef)


def _s5_combine_kernel(yf_ref, yb_ref, u_ref, d_ref, o_ref):
    y = yf_ref[...] + yb_ref[...] + d_ref[...] * u_ref[...].astype(F32)
    inner = math.sqrt(2.0 / math.pi) * (y + 0.044715 * (y * y * y))
    o_ref[...] = (0.5 * y * (1.0 + jnp.tanh(inner))).astype(o_ref.dtype)


def _s5_combine(yf, yb, p, d_skip, bm=256):
    m, d = yf.shape
    return pl.pallas_call(
        _s5_combine_kernel,
        grid=(m // bm,),
        in_specs=[pl.BlockSpec((bm, d), lambda i: (i, 0)),
                  pl.BlockSpec((bm, d), lambda i: (i, 0)),
                  pl.BlockSpec((bm, d), lambda i: (i, 0)),
                  pl.BlockSpec((1, d), lambda i: (0, 0))],
        out_specs=pl.BlockSpec((bm, d), lambda i: (i, 0)),
        out_shape=jax.ShapeDtypeStruct((m, d), BF16),
        compiler_params=_params(("arbitrary",)),
        name="s5_combine",
    )(yf, yb, p, d_skip.reshape(1, d))


def _s5_operators(A_re, A_im, log_dt, B_re, B_im, C_re, C_im):
    hi = lax.Precision.HIGHEST
    dt = jnp.exp(log_dt)[..., None]
    mag = jnp.exp(A_re * dt)
    a_re = mag * jnp.cos(A_im * dt)
    a_im = mag * jnp.sin(A_im * dt)
    den = A_re * A_re + A_im * A_im
    f_re = ((a_re - 1.0) * A_re + a_im * A_im) / den
    f_im = (a_im * A_re - (a_re - 1.0) * A_im) / den
    bb_re = f_re[..., None] * B_re - f_im[..., None] * B_im
    bb_im = f_re[..., None] * B_im + f_im[..., None] * B_re

    pr = [jnp.ones_like(a_re)]
    pi = [jnp.zeros_like(a_im)]
    for _ in range(CHUNK):
        pr_n = pr[-1] * a_re - pi[-1] * a_im
        pi_n = pr[-1] * a_im + pi[-1] * a_re
        pr.append(pr_n)
        pi.append(pi_n)
    pw_re = jnp.stack(pr, axis=2)
    pw_im = jnp.stack(pi, axis=2)

    ca_re = C_re[:, :, None] * pw_re[:, :, :, None, :] - C_im[:, :, None] * pw_im[:, :, :, None, :]
    ca_im = C_re[:, :, None] * pw_im[:, :, :, None, :] + C_im[:, :, None] * pw_re[:, :, :, None, :]

    kk = (jnp.einsum('dgkcp,dgpe->dgkce', ca_re[:, :, :CHUNK], bb_re, precision=hi)
          - jnp.einsum('dgkcp,dgpe->dgkce', ca_im[:, :, :CHUNK], bb_im, precision=hi))
    t_idx = jnp.arange(CHUNK)
    lag = t_idx[None, :] - t_idx[:, None]
    kt = kk.transpose(0, 1, 2, 4, 3)
    m = kt[:, :, jnp.clip(lag, 0, CHUNK - 1)]
    m = jnp.where((lag >= 0)[None, None, :, :, None, None], m, 0.0)
    m = m.transpose(0, 1, 2, 4, 3, 5).reshape(2, S5_GROUPS, CHUNK_W, CHUNK_W)

    rev_re = pw_re[:, :, CHUNK - 1::-1][:, :, :CHUNK]
    rev_im = pw_im[:, :, CHUNK - 1::-1][:, :, :CHUNK]
    bt_re = bb_re.transpose(0, 1, 3, 2)
    bt_im = bb_im.transpose(0, 1, 3, 2)
    s_re = rev_re[:, :, :, None, :] * bt_re[:, :, None] - rev_im[:, :, :, None, :] * bt_im[:, :, None]
    s_im = rev_re[:, :, :, None, :] * bt_im[:, :, None] + rev_im[:, :, :, None, :] * bt_re[:, :, None]
    s_re = s_re.reshape(2, S5_GROUPS, CHUNK_W, S5_STATE)
    s_im = s_im.reshape(2, S5_GROUPS, CHUNK_W, S5_STATE)
    s_all = jnp.concatenate([s_re, s_im, s_im, s_re], axis=-1)

    r_re = ca_re[:, :, 1:].transpose(0, 1, 4, 2, 3).reshape(2, S5_GROUPS, S5_STATE, CHUNK_W)
    r_im = -ca_im[:, :, 1:].transpose(0, 1, 4, 2, 3).reshape(2, S5_GROUPS, S5_STATE, CHUNK_W)
    r_all = jnp.concatenate([r_re, r_im], axis=2)

    ar = pw_re[:, :, CHUNK]
    ai = pw_im[:, :, CHUNK]
    c_a = jnp.concatenate([ar, ar, ar, ar], axis=-1)
    c_b = jnp.concatenate([-ai, ai, ai, -ai], axis=-1)
    coef = jnp.stack([c_a, c_b], axis=2)
    coef = jnp.repeat(coef.transpose(1, 2, 0, 3), V7X_SUBLANES // 2, axis=2)

    w1 = jnp.concatenate([s_all[0], s_all[1]], axis=-1).astype(BF16)
    w2 = jnp.concatenate([m[0], m[1]], axis=-1).astype(BF16)
    r = jnp.concatenate([r_all[0], r_all[1]], axis=-1).astype(BF16)
    return w1, w2, r, coef


def _rope_tables(n_tokens):
    rows = n_tokens // GRID_W
    row = jnp.repeat(jnp.arange(rows), GRID_W).astype(F32)
    col = jnp.tile(jnp.arange(GRID_W), rows).astype(F32)
    n_freq = HEAD_DIM // 4
    inv_freq = ROPE_BASE ** (-jnp.arange(n_freq, dtype=F32) / n_freq)
    ar = row[:, None] * inv_freq
    ac = col[:, None] * inv_freq
    cos_t = jnp.concatenate([jnp.cos(ar), jnp.cos(ar), jnp.cos(ac), jnp.cos(ac)], axis=1)
    sin_t = jnp.concatenate([-jnp.sin(ar), jnp.sin(ar), -jnp.sin(ac), jnp.sin(ac)], axis=1)
    return cos_t, sin_t


def kernel(x, c, ctx, c_ctx, ada_w, ada_b, norm_pre, norm_post, attn_w_in, attn_w_out, attn_lam, attn_subln, s5_w_in, s5_A_re, s5_A_im, s5_log_dt, s5_B_re, s5_B_im, s5_C_re, s5_C_im, s5_D, s5_w_glu, s5_w_out):
    bsz, seq, d = x.shape
    n_ctx = ctx.shape[1]
    depth = ada_w.shape[0]
    assert d == D_MODEL and depth == 2 and bsz + 1 <= MOD_ROWS
    ctx_row = bsz

    cc = jnp.concatenate([c, c_ctx[None, :], jnp.zeros((MOD_ROWS - bsz - 1, d), F32)], axis=0)
    mods = _modulation(cc, ada_w, ada_b)
    mods = mods.reshape(depth, MOD_ROWS, 3, 1, d).transpose(0, 2, 1, 3, 4)

    x_lat = x.reshape(bsz * seq, d)
    x_ctx = ctx.reshape(bsz * n_ctx, d)
    norm_bm = 256
    lat_tiles = seq // norm_bm

    shift, scale, gate = mods[0, 0], mods[0, 1], mods[0, 2]
    h_lat = _normmod(x_lat, norm_pre[0], scale, shift, tiles_per_batch=lat_tiles, bm=norm_bm)
    h_ctx = _normmod(x_ctx, norm_pre[0], scale, shift, fixed_row=ctx_row, bm=norm_bm)

    w_in = attn_w_in[0].astype(BF16)
    w_qk, w_vg = w_in[:, :2 * QK_W], w_in[:, 2 * QK_W:]
    cos_t, sin_t = _rope_tables(seq)
    qk_lat = _matmul_qk(h_lat, w_qk, cos_t, sin_t, rope=True)
    vg_lat = _matmul(h_lat, w_vg)
    dummy = jnp.zeros((V7X_SUBLANES, HEAD_DIM), F32)
    qk_ctx = _matmul_qk(h_ctx, w_qk, dummy, dummy, rope=False)
    vg_ctx = _matmul(h_ctx, w_vg)

    lam_init = 0.8 - 0.6 * math.exp(-0.3 * 0)
    a_lat = _attention(attn_lam[0], attn_subln[0], qk_lat, vg_lat,
                       [(qk_ctx, vg_ctx, n_ctx), (qk_lat, vg_lat, seq)],
                       batch=bsz, n_q=seq, tq=256, lam_init=lam_init)
    a_ctx = _attention(attn_lam[0], attn_subln[0], qk_ctx, vg_ctx,
                       [(qk_ctx, vg_ctx, n_ctx)],
                       batch=bsz, n_q=n_ctx, tq=n_ctx, lam_init=lam_init)
    w_out = attn_w_out[0].astype(BF16)
    o_lat = _matmul(a_lat, w_out)
    o_ctx = _matmul(a_ctx, w_out)
    x_lat = _resid(x_lat, o_lat, norm_post[0], gate, tiles_per_batch=lat_tiles, bm=norm_bm)
    x_ctx = _resid(x_ctx, o_ctx, norm_post[0], gate, fixed_row=ctx_row, bm=norm_bm)

    shift, scale, gate = mods[1, 0], mods[1, 1], mods[1, 2]
    h_lat = _normmod(x_lat, norm_pre[1], scale, shift, tiles_per_batch=lat_tiles, bm=norm_bm)
    h_ctx = _normmod(x_ctx, norm_pre[1], scale, shift, fixed_row=ctx_row, bm=norm_bm)
    w_in = s5_w_in[0].astype(BF16)
    p_lat = _matmul(h_lat, w_in)
    u_ctx = _matmul(h_ctx, w_in[:, :d])

    u_lat3 = p_lat[:, :d].reshape(bsz, seq, d)
    u_ctx3 = u_ctx.reshape(bsz, n_ctx, d)
    seq_f = jnp.concatenate([u_ctx3, u_lat3], axis=1)
    seq_b = jnp.concatenate([u_ctx3[:, ::-1], u_lat3[:, ::-1]], axis=1)
    n_chunks = (n_ctx + seq) // CHUNK
    n_ctx_chunks = n_ctx // CHUNK
    useq = jnp.stack([seq_f, seq_b], axis=0)
    useq = useq.reshape(2, bsz, n_chunks, CHUNK, S5_GROUPS, S5_GROUP)
    useq = useq.transpose(4, 2, 0, 1, 3, 5).reshape(S5_GROUPS, n_chunks * 2 * bsz, CHUNK_W)

    w1, w2, r, coef = _s5_operators(s5_A_re[0], s5_A_im[0], s5_log_dt[0], s5_B_re[0], s5_B_im[0],
                                    s5_C_re[0], s5_C_im[0])
    y = _s5_scan(useq, w1, w2, r, coef, n_chunks=n_chunks, n_ctx_chunks=n_ctx_chunks)
    n_lat_chunks = n_chunks - n_ctx_chunks
    y = y.reshape(S5_GROUPS, n_lat_chunks, 2, bsz, CHUNK, S5_GROUP)
    y = y.transpose(2, 3, 1, 4, 0, 5).reshape(2, bsz, seq, d)
    yf = y[0].reshape(bsz * seq, d)
    yb = y[1][:, ::-1].reshape(bsz * seq, d)

    yact = _s5_combine(yf, yb, p_lat, s5_D[0])
    gl = _matmul_glu(yact, s5_w_glu[0].astype(BF16), p_lat)
    o_lat = _matmul(gl, s5_w_out[0].astype(BF16))
    x_lat = _resid(x_lat, o_lat, norm_post[1], gate, tiles_per_batch=lat_tiles, bm=norm_bm)
    return x_lat.reshape(bsz, seq, d)
```

```python
import functools
import math

import jax
import jax.numpy as jnp
from jax import lax
from jax.experimental import pallas as pl
from jax.experimental.pallas import tpu as pltpu

F32 = jnp.float32
BF16 = jnp.bfloat16

V7X_LANES = 128
V7X_SUBLANES = 8
V7X_VMEM_BYTES = 64 * 1024 * 1024
VMEM_LIMIT = V7X_VMEM_BYTES - 8 * 1024 * 1024

D_MODEL = 4096
GRID_W = 64
EPS = 1e-6
HEADS = 16
HEAD_DIM = 128
V_DIM = 2 * HEAD_DIM
QK_W = HEADS * 2 * HEAD_DIM
ROPE_BASE = 10000.0
S5_GROUP = 16
S5_GROUPS = D_MODEL // S5_GROUP
S5_STATE = 64
CHUNK = 16
CHUNK_W = CHUNK * S5_GROUP
MOD_ROWS = 8


def _params(sem, vmem=VMEM_LIMIT):
    return pltpu.CompilerParams(dimension_semantics=sem, vmem_limit_bytes=vmem)


def _sigmoid(x):
    return 1.0 / (1.0 + jnp.exp(-x))


def _silu(x):
    return x * _sigmoid(x)


def _mod_kernel(c_ref, w_ref, b_ref, o_ref):
    a = _silu(c_ref[...])
    acc = jnp.dot(a, w_ref[0], preferred_element_type=F32, precision=lax.Precision.HIGHEST)
    o_ref[0] = acc + b_ref[0]


def _modulation(cc, ada_w, ada_b):
    depth, d, n = ada_w.shape
    bn = 512
    return pl.pallas_call(
        _mod_kernel,
        grid=(depth, n // bn),
        in_specs=[
            pl.BlockSpec((MOD_ROWS, d), lambda i, j: (0, 0)),
            pl.BlockSpec((1, d, bn), lambda i, j: (i, 0, j)),
            pl.BlockSpec((1, 1, bn), lambda i, j: (i, 0, j)),
        ],
        out_specs=pl.BlockSpec((1, MOD_ROWS, bn), lambda i, j: (i, 0, j)),
        out_shape=jax.ShapeDtypeStruct((depth, MOD_ROWS, n), F32),
        compiler_params=_params(("arbitrary", "arbitrary")),
        name="modulation",
    )(cc, ada_w, ada_b.reshape(depth, 1, n))


def _rms(x):
    return x * lax.rsqrt(jnp.mean(x * x, axis=-1, keepdims=True) + EPS)


def _normmod_kernel(x_ref, g_ref, scale_ref, shift_ref, o_ref):
    y = _rms(x_ref[...]) * g_ref[...]
    o_ref[...] = (y * (1.0 + scale_ref[0]) + shift_ref[0]).astype(o_ref.dtype)


def _resid_kernel(x_ref, o_in_ref, g_ref, gate_ref, o_ref):
    y = _rms(o_in_ref[...].astype(F32)) * g_ref[...]
    o_ref[...] = x_ref[...] + gate_ref[0] * y


def _mod_row_map(rows_per_mod_row, fixed_row):
    if fixed_row is not None:
        return lambda i: (fixed_row, 0, 0)
    return lambda i: (i // rows_per_mod_row, 0, 0)


def _normmod(x2, g, scale3, shift3, *, tiles_per_batch=None, fixed_row=None, bm=256):
    m, d = x2.shape
    row_map = _mod_row_map(tiles_per_batch, fixed_row)
    return pl.pallas_call(
        _normmod_kernel,
        grid=(m // bm,),
        in_specs=[
            pl.BlockSpec((bm, d), lambda i: (i, 0)),
            pl.BlockSpec((1, d), lambda i: (0, 0)),
            pl.BlockSpec((1, 1, d), row_map),
            pl.BlockSpec((1, 1, d), row_map),
        ],
        out_specs=pl.BlockSpec((bm, d), lambda i: (i, 0)),
        out_shape=jax.ShapeDtypeStruct((m, d), BF16),
        compiler_params=_params(("arbitrary",)),
        name="normmod",
    )(x2, g.reshape(1, d), scale3, shift3)


def _resid(x2, o2, g, gate3, *, tiles_per_batch=None, fixed_row=None, bm=256):
    m, d = x2.shape
    row_map = _mod_row_map(tiles_per_batch, fixed_row)
    return pl.pallas_call(
        _resid_kernel,
        grid=(m // bm,),
        in_specs=[
            pl.BlockSpec((bm, d), lambda i: (i, 0)),
            pl.BlockSpec((bm, d), lambda i: (i, 0)),
            pl.BlockSpec((1, d), lambda i: (0, 0)),
            pl.BlockSpec((1, 1, d), row_map),
        ],
        out_specs=pl.BlockSpec((bm, d), lambda i: (i, 0)),
        out_shape=jax.ShapeDtypeStruct((m, d), F32),
        compiler_params=_params(("arbitrary",)),
        name="resid",
    )(x2, o2, g.reshape(1, d), gate3)


def _mm_kernel(x_ref, w_ref, o_ref):
    acc = jnp.dot(x_ref[...], w_ref[...], preferred_element_type=F32)
    o_ref[...] = acc.astype(o_ref.dtype)


def _mm_qk_kernel(x_ref, w_ref, cos_ref, sin_ref, o_ref, *, rope, q_tiles, q_scale):
    acc = jnp.dot(x_ref[...], w_ref[...], preferred_element_type=F32)
    sc = jnp.where(pl.program_id(1) < q_tiles, q_scale, 1.0).astype(F32)
    bn = acc.shape[1]
    if rope:
        cos = cos_ref[...]
        sin = sin_ref[...]
        lane = lax.broadcasted_iota(jnp.int32, (1, HEAD_DIM), 1)
        first = (lane % (HEAD_DIM // 2)) < (HEAD_DIM // 4)
    for c in range(bn // HEAD_DIM):
        xc = acc[:, c * HEAD_DIM:(c + 1) * HEAD_DIM]
        if rope:
            partner = jnp.where(first,
                                pltpu.roll(xc, HEAD_DIM - HEAD_DIM // 4, 1),
                                pltpu.roll(xc, HEAD_DIM // 4, 1))
            xc = xc * cos + partner * sin
        o_ref[:, c * HEAD_DIM:(c + 1) * HEAD_DIM] = (xc * sc).astype(o_ref.dtype)


def _mm_glu_kernel(x_ref, w_ref, y_ref, z_ref, o_ref):
    acc = jnp.dot(x_ref[...], w_ref[...], preferred_element_type=F32)
    y = y_ref[...].astype(F32)
    o_ref[...] = (y * _sigmoid(acc) * _silu(z_ref[...].astype(F32))).astype(o_ref.dtype)


def _mm_tiles(m, n):
    return min(m, 1024), min(n, 1024)


def _matmul(x, w, out_dtype=BF16):
    m, k = x.shape
    n = w.shape[1]
    bm, bn = _mm_tiles(m, n)
    return pl.pallas_call(
        _mm_kernel,
        grid=(m // bm, n // bn),
        in_specs=[pl.BlockSpec((bm, k), lambda i, j: (i, 0)),
                  pl.BlockSpec((k, bn), lambda i, j: (0, j))],
        out_specs=pl.BlockSpec((bm, bn), lambda i, j: (i, j)),
        out_shape=jax.ShapeDtypeStruct((m, n), out_dtype),
        compiler_params=_params(("arbitrary", "arbitrary")),
        name="proj",
    )(x, w)


def _matmul_qk(x, w, cos_t, sin_t, *, rope):
    m, k = x.shape
    n = w.shape[1]
    bm, bn = _mm_tiles(m, n)
    tiles_per_seq = cos_t.shape[0] // bm if rope else 1
    tab_map = (lambda i, j: (i % tiles_per_seq, 0)) if rope else (lambda i, j: (0, 0))
    tab_rows = bm if rope else V7X_SUBLANES
    kern = functools.partial(_mm_qk_kernel, rope=rope, q_tiles=(n // 2) // bn,
                             q_scale=HEAD_DIM ** -0.5)
    return pl.pallas_call(
        kern,
        grid=(m // bm, n // bn),
        in_specs=[pl.BlockSpec((bm, k), lambda i, j: (i, 0)),
                  pl.BlockSpec((k, bn), lambda i, j: (0, j)),
                  pl.BlockSpec((tab_rows, HEAD_DIM), tab_map),
                  pl.BlockSpec((tab_rows, HEAD_DIM), tab_map)],
        out_specs=pl.BlockSpec((bm, bn), lambda i, j: (i, j)),
        out_shape=jax.ShapeDtypeStruct((m, n), BF16),
        compiler_params=_params(("arbitrary", "arbitrary")),
        name="proj_qk_rope" if rope else "proj_qk",
    )(x, w, cos_t, sin_t)


def _matmul_glu(y, w, p):
    m, k = y.shape
    n = w.shape[1]
    bm, bn = _mm_tiles(m, n)
    z_off = n // bn
    return pl.pallas_call(
        _mm_glu_kernel,
        grid=(m // bm, n // bn),
        in_specs=[pl.BlockSpec((bm, k), lambda i, j: (i, 0)),
                  pl.BlockSpec((k, bn), lambda i, j: (0, j)),
                  pl.BlockSpec((bm, bn), lambda i, j: (i, j)),
                  pl.BlockSpec((bm, bn), lambda i, j: (i, z_off + j))],
        out_specs=pl.BlockSpec((bm, bn), lambda i, j: (i, j)),
        out_shape=jax.ShapeDtypeStruct((m, n), BF16),
        compiler_params=_params(("arbitrary", "arbitrary")),
        name="proj_glu",
    )(y, w, y, p)


def _attn_kernel(lam_ref, g_ref, q_ref, gate_ref, *refs, n_seg, lam_init):
    o_ref = refs[2 * n_seg]
    segs = [(refs[2 * i], refs[2 * i + 1]) for i in range(n_seg)]
    lv = lam_ref[...]
    t1 = jnp.sum(lv[0:1] * lv[1:2], axis=1, keepdims=True)
    t2 = jnp.sum(lv[2:3] * lv[3:4], axis=1, keepdims=True)
    lam = jnp.exp(t1) - jnp.exp(t2) + lam_init

    q1 = q_ref[:, :HEAD_DIM]
    q2 = q_ref[:, HEAD_DIM:]
    dn = (((1,), (1,)), ((), ()))
    s1 = [lax.dot_general(q1, k[:, :HEAD_DIM], dn, preferred_element_type=F32) for k, _ in segs]
    s2 = [lax.dot_general(q2, k[:, HEAD_DIM:], dn, preferred_element_type=F32) for k, _ in segs]

    def softmax_parts(ss):
        m = functools.reduce(jnp.maximum, [jnp.max(s, axis=1, keepdims=True) for s in ss])
        ps = [jnp.exp(s - m) for s in ss]
        l = functools.reduce(jnp.add, [jnp.sum(p, axis=1, keepdims=True) for p in ps])
        return ps, l

    p1, l1 = softmax_parts(s1)
    p2, l2 = softmax_parts(s2)
    c1 = 1.0 / l1
    c2 = lam / l2
    o = None
    for a, b, (_, v) in zip(p1, p2, segs):
        w = (a * c1 - b * c2).astype(BF16)
        part = jnp.dot(w, v[...], preferred_element_type=F32)
        o = part if o is None else o + part
    o = _rms(o) * g_ref[...] * (1.0 - lam_init)
    o_ref[...] = (o * _silu(gate_ref[...].astype(F32))).astype(o_ref.dtype)


def _attention(lam_vecs, subln_g, q_arr, gate_arr, segs, *, batch, n_q, tq, lam_init):
    nq_tiles = n_q // tq
    n_seg = len(segs)
    in_specs = [
        pl.BlockSpec((4, HEAD_DIM), lambda b, h, i: (0, 0)),
        pl.BlockSpec((1, V_DIM), lambda b, h, i: (0, 0)),
        pl.BlockSpec((tq, V_DIM), lambda b, h, i: (b * nq_tiles + i, h)),
        pl.BlockSpec((tq, V_DIM), lambda b, h, i: (b * nq_tiles + i, HEADS + h)),
    ]
    args = [lam_vecs, subln_g.reshape(1, V_DIM), q_arr, gate_arr]
    for k_arr, v_arr, rows in segs:
        in_specs.append(pl.BlockSpec((rows, V_DIM), lambda b, h, i: (b, HEADS + h)))
        in_specs.append(pl.BlockSpec((rows, V_DIM), lambda b, h, i: (b, h)))
        args += [k_arr, v_arr]
    kern = functools.partial(_attn_kernel, n_seg=n_seg, lam_init=lam_init)
    return pl.pallas_call(
        kern,
        grid=(batch, HEADS, nq_tiles),
        in_specs=in_specs,
        out_specs=pl.BlockSpec((tq, V_DIM), lambda b, h, i: (b * nq_tiles + i, h)),
        out_shape=jax.ShapeDtypeStruct((batch * n_q, HEADS * V_DIM), BF16),
        compiler_params=_params(("arbitrary", "arbitrary", "arbitrary")),
        name="diff_attn",
    )(*args)


S5_BLOCK_GROUPS = V7X_LANES // S5_GROUP
S5_PASS_GROUPS = 4
S5_ROW_CHUNK = 32
_DN_LAST = (((1,), (1,)), ((), ()))


def _shift_lanes_256(x, n, lane):
    a, b = x[:, :V7X_LANES], x[:, V7X_LANES:]
    zero = jnp.zeros_like(a)
    if n == 0:
        return x
    if n > 0:
        q, r = divmod(n, V7X_LANES)
        if r == 0:
            lo, hi = zero, a
        else:
            ra, rb = pltpu.roll(a, r, 1), pltpu.roll(b, r, 1)
            keep = lane >= r
            if q == 0:
                lo, hi = jnp.where(keep, ra, 0.0), jnp.where(keep, rb, ra)
            else:
                lo, hi = zero, jnp.where(keep, ra, 0.0)
    else:
        q, r = divmod(-n, V7X_LANES)
        if r == 0:
            lo, hi = b, zero
        else:
            ra, rb = pltpu.roll(a, V7X_LANES - r, 1), pltpu.roll(b, V7X_LANES - r, 1)
            keep = lane < V7X_LANES - r
            if q == 0:
                lo, hi = jnp.where(keep, ra, rb), jnp.where(keep, rb, 0.0)
            else:
                lo, hi = jnp.where(keep, rb, 0.0), zero
    return jnp.concatenate([lo, hi], axis=1)


def _gelu_tanh(y):
    inner = math.sqrt(2.0 / math.pi) * (y + 0.044715 * (y * y * y))
    return 0.5 * y * (1.0 + jnp.tanh(inner))


def _s5_kernel(uctx_ref, ulat_ref, d_ref, st_ref, rt_ref, kc_ref, bt_ref, coef_ref, o_ref,
               u32, u_scr, s_scr, y_scr, m_scr, *, batch, n_ctx_chunks, n_lat_chunks):
    ng = S5_BLOCK_GROUPS
    rc = S5_ROW_CHUNK
    ctx_rows = batch * n_ctx_chunks
    lat_rows = batch * n_lat_chunks
    rows = ctx_rows + lat_rows
    lane = lax.broadcasted_iota(jnp.int32, (1, V7X_LANES), 1)
    piece = [(lane >= S5_GROUP * k) & (lane < S5_GROUP * (k + 1)) for k in range(ng)]

    u32[0:ctx_rows * CHUNK, :] = uctx_ref[...].astype(F32)
    u32[ctx_rows * CHUNK:, :] = ulat_ref[...].astype(F32)

    def gather(i, carry):
        r0 = pl.multiple_of(i * rc, rc)
        for h in range(2):
            vs = [u32[pl.ds(r0 * CHUNK + ng * h + k, rc, stride=CHUNK), :] for k in range(ng)]
            for g in range(ng):
                acc = None
                for k in range(ng):
                    sh = (S5_GROUP * (k - g)) % V7X_LANES
                    xr = vs[k] if sh == 0 else pltpu.roll(vs[k], sh, 1)
                    acc = xr if acc is None else jnp.where(piece[k], xr, acc)
                u_scr[g, pl.ds(r0, rc), V7X_LANES * h:V7X_LANES * (h + 1)] = acc.astype(BF16)
        return carry

    lax.fori_loop(0, rows // rc, gather, 0)

    for g in range(ng):
        kf = lax.dot_general(bt_ref[g, 0], kc_ref[g, 0], _DN_LAST, preferred_element_type=F32,
                             precision=lax.Precision.HIGHEST)
        kb = lax.dot_general(bt_ref[g, 1], kc_ref[g, 1], _DN_LAST, preferred_element_type=F32,
                             precision=lax.Precision.HIGHEST)
        for s in range(CHUNK):
            blk = (_shift_lanes_256(kf, S5_GROUP * s, lane)
                   + _shift_lanes_256(kb, -S5_GROUP * (CHUNK - 1 - s), lane))
            m_scr[g, S5_GROUP * s:S5_GROUP * (s + 1), :] = blk.astype(BF16)

    for p in range(ng // S5_PASS_GROUPS):
        g0 = p * S5_PASS_GROUPS
        for gi in range(S5_PASS_GROUPS):
            o1 = lax.dot_general(u_scr[g0 + gi], st_ref[g0 + gi], _DN_LAST, preferred_element_type=F32)
            for cb in range(4):
                s_scr[gi, cb] = o1[:, V7X_LANES * cb:V7X_LANES * (cb + 1)]

        coefs = [[coef_ref[g0 + gi, i:i + 1, :] for i in range(4)] for gi in range(S5_PASS_GROUPS)]

        def make_step(base, n):
            def step(i, carry):
                out = []
                for gi in range(S5_PASS_GROUPS):
                    hf, hsf, hb, hsb = carry[4 * gi:4 * gi + 4]
                    rf = pl.ds(base + i, batch, stride=n)
                    rb = pl.ds(base + n - 1 - i, batch, stride=n)
                    sf, ssf = s_scr.at[gi, 0][rf, :], s_scr.at[gi, 1][rf, :]
                    sb, ssb = s_scr.at[gi, 2][rb, :], s_scr.at[gi, 3][rb, :]
                    s_scr.at[gi, 0][rf, :] = hf
                    s_scr.at[gi, 2][rb, :] = hb
                    caf, cbf, cab, cbb = coefs[gi]
                    out += [caf * hf + cbf * hsf + sf, caf * hsf - cbf * hf + ssf,
                            cab * hb + cbb * hsb + sb, cab * hsb - cbb * hb + ssb]
                return tuple(out)
            return step

        carry = tuple(jnp.zeros((batch, V7X_LANES), F32) for _ in range(4 * S5_PASS_GROUPS))
        carry = lax.fori_loop(0, n_ctx_chunks, make_step(0, n_ctx_chunks), carry)
        lax.fori_loop(0, n_lat_chunks, make_step(ctx_rows, n_lat_chunks), carry)

        for gi in range(S5_PASS_GROUPS):
            g = g0 + gi
            hin = jnp.concatenate([s_scr[gi, 0, ctx_rows:, :], s_scr[gi, 2, ctx_rows:, :]], axis=1)
            y = jnp.dot(u_scr[g, ctx_rows:, :], m_scr[g], preferred_element_type=F32)
            y = y + lax.dot_general(hin.astype(BF16), rt_ref[g], _DN_LAST, preferred_element_type=F32)
            y_scr[g] = y

    d = d_ref[...]

    def scatter(i, carry):
        r0 = pl.multiple_of(i * rc, rc)
        for h in range(2):
            ys = [y_scr[g, pl.ds(r0, rc), V7X_LANES * h:V7X_LANES * (h + 1)] for g in range(ng)]
            for k in range(ng):
                acc = None
                for g in range(ng):
                    sh = (S5_GROUP * (g - k)) % V7X_LANES
                    xr = ys[g] if sh == 0 else pltpu.roll(ys[g], sh, 1)
                    acc = xr if acc is None else jnp.where(piece[g], xr, acc)
                rows_t = pl.ds((ctx_rows + r0) * CHUNK + ng * h + k, rc, stride=CHUNK)
                u32[rows_t, :] = _gelu_tanh(acc + d * u32[rows_t, :])
        return carry

    lax.fori_loop(0, lat_rows // rc, scatter, 0)
    o_ref[...] = u32[ctx_rows * CHUNK:, :].astype(o_ref.dtype)


def _s5_mix(u_ctx, p_lat, d_skip, st, rt, kc, bt, coef, *, batch, n_ctx, seq):
    ng = S5_BLOCK_GROUPS
    width = u_ctx.shape[1]
    n_ctx_chunks, n_lat_chunks = n_ctx // CHUNK, seq // CHUNK
    rows = batch * (n_ctx_chunks + n_lat_chunks)
    lat_rows = batch * n_lat_chunks
    kern = functools.partial(_s5_kernel, batch=batch, n_ctx_chunks=n_ctx_chunks, n_lat_chunks=n_lat_chunks)
    one = pl.Buffered(1)
    return pl.pallas_call(
        kern,
        grid=(width // V7X_LANES,),
        in_specs=[
            pl.BlockSpec((batch * n_ctx, V7X_LANES), lambda i: (0, i)),
            pl.BlockSpec((batch * seq, V7X_LANES), lambda i: (0, i), pipeline_mode=one),
            pl.BlockSpec((1, V7X_LANES), lambda i: (0, i)),
            pl.BlockSpec((ng,) + st.shape[1:], lambda i: (i, 0, 0)),
            pl.BlockSpec((ng,) + rt.shape[1:], lambda i: (i, 0, 0)),
            pl.BlockSpec((ng,) + kc.shape[1:], lambda i: (i, 0, 0, 0)),
            pl.BlockSpec((ng,) + bt.shape[1:], lambda i: (i, 0, 0, 0)),
            pl.BlockSpec((ng,) + coef.shape[1:], lambda i: (i, 0, 0)),
        ],
        out_specs=pl.BlockSpec((batch * seq, V7X_LANES), lambda i: (0, i)),
        out_shape=jax.ShapeDtypeStruct((batch * seq, width), BF16),
        scratch_shapes=[
            pltpu.VMEM((rows * CHUNK, V7X_LANES), F32),
            pltpu.VMEM((ng, rows, CHUNK_W), BF16),
            pltpu.VMEM((S5_PASS_GROUPS, 4, rows, V7X_LANES), F32),
            pltpu.VMEM((ng, lat_rows, CHUNK_W), F32),
            pltpu.VMEM((ng, CHUNK_W, CHUNK_W), BF16),
        ],
        compiler_params=_params(("arbitrary",), vmem=V7X_VMEM_BYTES - 6 * 1024 * 1024),
        name="s5_mix",
    )(u_ctx, p_lat, d_skip.reshape(1, width), st, rt, kc, bt, coef)


def _s5_tables(A_re, A_im, log_dt, B_re, B_im, C_re, C_im):
    g = A_re.shape[1]
    dt = jnp.exp(log_dt)[..., None]
    mag = jnp.exp(A_re * dt)
    a_re = mag * jnp.cos(A_im * dt)
    a_im = mag * jnp.sin(A_im * dt)
    den = A_re * A_re + A_im * A_im
    f_re = ((a_re - 1.0) * A_re + a_im * A_im) / den
    f_im = (a_im * A_re - (a_re - 1.0) * A_im) / den
    bb_re = f_re[..., None] * B_re - f_im[..., None] * B_im
    bb_im = f_re[..., None] * B_im + f_im[..., None] * B_re

    pr, pi = [jnp.ones_like(a_re)], [jnp.zeros_like(a_im)]
    for _ in range(CHUNK):
        pr, pi = pr + [pr[-1] * a_re - pi[-1] * a_im], pi + [pr[-1] * a_im + pi[-1] * a_re]
    pw_re, pw_im = jnp.stack(pr, axis=2), jnp.stack(pi, axis=2)

    ca_re = C_re[:, :, None] * pw_re[:, :, :, None, :] - C_im[:, :, None] * pw_im[:, :, :, None, :]
    ca_im = C_re[:, :, None] * pw_im[:, :, :, None, :] + C_im[:, :, None] * pw_re[:, :, :, None, :]
    ca = jnp.concatenate([ca_re, -ca_im], axis=-1)

    kc = jnp.stack([ca[0][:, :CHUNK], ca[1][:, CHUNK - 1::-1]], axis=1).reshape(g, 2, CHUNK_W, 2 * S5_STATE)
    rt = jnp.concatenate([ca[0][:, 1:].reshape(g, CHUNK_W, 2 * S5_STATE),
                          ca[1][:, :0:-1].reshape(g, CHUNK_W, 2 * S5_STATE)], axis=-1).astype(BF16)
    bt = jnp.concatenate([bb_re.transpose(0, 1, 3, 2), bb_im.transpose(0, 1, 3, 2)], axis=-1)
    bt = bt.transpose(1, 0, 2, 3)

    def state_rows(d, powers):
        qr = pw_re[d][:, powers].transpose(0, 2, 1)[..., None]
        qi = pw_im[d][:, powers].transpose(0, 2, 1)[..., None]
        br, bi = bb_re[d][:, :, None, :], bb_im[d][:, :, None, :]
        s_re = (qr * br - qi * bi).reshape(g, S5_STATE, CHUNK_W)
        s_im = (qr * bi + qi * br).reshape(g, S5_STATE, CHUNK_W)
        return jnp.concatenate([s_re, s_im, s_im, s_re], axis=1)

    k_idx = jnp.arange(CHUNK)
    st = jnp.concatenate([state_rows(0, CHUNK - 1 - k_idx), state_rows(1, k_idx)], axis=1).astype(BF16)

    ar, ai = pw_re[:, :, CHUNK], pw_im[:, :, CHUNK]
    c_a = jnp.concatenate([ar, ar], axis=-1)
    c_b = jnp.concatenate([-ai, ai], axis=-1)
    coef = jnp.stack([c_a[0], c_b[0], c_a[1], c_b[1]], axis=1)
    return st, rt, kc, bt, coef


def _rope_tables(n_tokens):
    rows = n_tokens // GRID_W
    row = jnp.repeat(jnp.arange(rows), GRID_W).astype(F32)
    col = jnp.tile(jnp.arange(GRID_W), rows).astype(F32)
    n_freq = HEAD_DIM // 4
    inv_freq = ROPE_BASE ** (-jnp.arange(n_freq, dtype=F32) / n_freq)
    ar = row[:, None] * inv_freq
    ac = col[:, None] * inv_freq
    cos_t = jnp.concatenate([jnp.cos(ar), jnp.cos(ar), jnp.cos(ac), jnp.cos(ac)], axis=1)
    sin_t = jnp.concatenate([-jnp.sin(ar), jnp.sin(ar), -jnp.sin(ac), jnp.sin(ac)], axis=1)
    return cos_t, sin_t


def kernel(x, c, ctx, c_ctx, ada_w, ada_b, norm_pre, norm_post, attn_w_in, attn_w_out, attn_lam, attn_subln, s5_w_in, s5_A_re, s5_A_im, s5_log_dt, s5_B_re, s5_B_im, s5_C_re, s5_C_im, s5_D, s5_w_glu, s5_w_out):
    bsz, seq, d = x.shape
    n_ctx = ctx.shape[1]
    depth = ada_w.shape[0]
    assert d == D_MODEL and depth == 2 and bsz + 1 <= MOD_ROWS
    ctx_row = bsz

    cc = jnp.concatenate([c, c_ctx[None, :], jnp.zeros((MOD_ROWS - bsz - 1, d), F32)], axis=0)
    mods = _modulation(cc, ada_w, ada_b)
    mods = mods.reshape(depth, MOD_ROWS, 3, 1, d).transpose(0, 2, 1, 3, 4)

    x_lat = x.reshape(bsz * seq, d)
    x_ctx = ctx.reshape(bsz * n_ctx, d)
    norm_bm = 256
    lat_tiles = seq // norm_bm

    shift, scale, gate = mods[0, 0], mods[0, 1], mods[0, 2]
    h_lat = _normmod(x_lat, norm_pre[0], scale, shift, tiles_per_batch=lat_tiles, bm=norm_bm)
    h_ctx = _normmod(x_ctx, norm_pre[0], scale, shift, fixed_row=ctx_row, bm=norm_bm)

    w_in = attn_w_in[0].astype(BF16)
    w_qk, w_vg = w_in[:, :2 * QK_W], w_in[:, 2 * QK_W:]
    cos_t, sin_t = _rope_tables(seq)
    qk_lat = _matmul_qk(h_lat, w_qk, cos_t, sin_t, rope=True)
    vg_lat = _matmul(h_lat, w_vg)
    dummy = jnp.zeros((V7X_SUBLANES, HEAD_DIM), F32)
    qk_ctx = _matmul_qk(h_ctx, w_qk, dummy, dummy, rope=False)
    vg_ctx = _matmul(h_ctx, w_vg)

    lam_init = 0.8 - 0.6 * math.exp(-0.3 * 0)
    a_lat = _attention(attn_lam[0], attn_subln[0], qk_lat, vg_lat,
                       [(qk_ctx, vg_ctx, n_ctx), (qk_lat, vg_lat, seq)],
                       batch=bsz, n_q=seq, tq=256, lam_init=lam_init)
    a_ctx = _attention(attn_lam[0], attn_subln[0], qk_ctx, vg_ctx,
                       [(qk_ctx, vg_ctx, n_ctx)],
                       batch=bsz, n_q=n_ctx, tq=n_ctx, lam_init=lam_init)
    w_out = attn_w_out[0].astype(BF16)
    o_lat = _matmul(a_lat, w_out)
    o_ctx = _matmul(a_ctx, w_out)
    x_lat = _resid(x_lat, o_lat, norm_post[0], gate, tiles_per_batch=lat_tiles, bm=norm_bm)
    x_ctx = _resid(x_ctx, o_ctx, norm_post[0], gate, fixed_row=ctx_row, bm=norm_bm)

    shift, scale, gate = mods[1, 0], mods[1, 1], mods[1, 2]
    h_lat = _normmod(x_lat, norm_pre[1], scale, shift, tiles_per_batch=lat_tiles, bm=norm_bm)
    h_ctx = _normmod(x_ctx, norm_pre[1], scale, shift, fixed_row=ctx_row, bm=norm_bm)
    w_in = s5_w_in[0].astype(BF16)
    p_lat = _matmul(h_lat, w_in)
    u_ctx = _matmul(h_ctx, w_in[:, :d])

    st, rt, kc, bt, coef = _s5_tables(s5_A_re[0], s5_A_im[0], s5_log_dt[0], s5_B_re[0], s5_B_im[0],
                                      s5_C_re[0], s5_C_im[0])
    yact = _s5_mix(u_ctx, p_lat, s5_D[0], st, rt, kc, bt, coef, batch=bsz, n_ctx=n_ctx, seq=seq)
    gl = _matmul_glu(yact, s5_w_glu[0].astype(BF16), p_lat)
    o_lat = _matmul(gl, s5_w_out[0].astype(BF16))
    x_lat = _resid(x_lat, o_lat, norm_post[1], gate, tiles_per_batch=lat_tiles, bm=norm_bm)
    return x_lat.reshape(bsz, seq, d)
```

```python
import functools
import math

import jax
import jax.numpy as jnp
from jax import lax
from jax.experimental import pallas as pl
from jax.experimental.pallas import tpu as pltpu

F32 = jnp.float32
BF16 = jnp.bfloat16

V7X_LANES = 128
V7X_SUBLANES = 8
V7X_VMEM_BYTES = 64 * 1024 * 1024
VMEM_LIMIT = V7X_VMEM_BYTES - 8 * 1024 * 1024

D_MODEL = 4096
GRID_W = 64
EPS = 1e-6
HEADS = 16
HEAD_DIM = 128
V_DIM = 2 * HEAD_DIM
QK_W = HEADS * 2 * HEAD_DIM
ROPE_BASE = 10000.0
S5_GROUP = 16
S5_GROUPS = D_MODEL // S5_GROUP
S5_STATE = 64
CHUNK = 16
CHUNK_W = CHUNK * S5_GROUP
MOD_ROWS = 8
_DN_LAST = (((1,), (1,)), ((), ()))


def _params(sem, vmem=VMEM_LIMIT):
    return pltpu.CompilerParams(dimension_semantics=sem, vmem_limit_bytes=vmem)


def _sigmoid(x):
    return 1.0 / (1.0 + jnp.exp(-x))


def _silu(x):
    return x * _sigmoid(x)


def _mod_kernel(c_ref, w_ref, b_ref, o_ref):
    a = _silu(c_ref[...])
    acc = jnp.dot(a, w_ref[0], preferred_element_type=F32, precision=lax.Precision.HIGHEST)
    o_ref[0] = acc + b_ref[0]


def _modulation(cc, ada_w, ada_b):
    depth, d, n = ada_w.shape
    bn = 512
    return pl.pallas_call(
        _mod_kernel,
        grid=(depth, n // bn),
        in_specs=[
            pl.BlockSpec((MOD_ROWS, d), lambda i, j: (0, 0)),
            pl.BlockSpec((1, d, bn), lambda i, j: (i, 0, j)),
            pl.BlockSpec((1, 1, bn), lambda i, j: (i, 0, j)),
        ],
        out_specs=pl.BlockSpec((1, MOD_ROWS, bn), lambda i, j: (i, 0, j)),
        out_shape=jax.ShapeDtypeStruct((depth, MOD_ROWS, n), F32),
        compiler_params=_params(("arbitrary", "arbitrary")),
        name="modulation",
    )(cc, ada_w, ada_b.reshape(depth, 1, n))


def _rms(x):
    return x * lax.rsqrt(jnp.mean(x * x, axis=-1, keepdims=True) + EPS)


def _normmod_kernel(x_ref, g_ref, scale_ref, shift_ref, o_ref):
    y = _rms(x_ref[...]) * g_ref[...]
    o_ref[...] = (y * (1.0 + scale_ref[0]) + shift_ref[0]).astype(o_ref.dtype)


def _resid_kernel(x_ref, o_in_ref, g_ref, gate_ref, o_ref):
    y = _rms(o_in_ref[...].astype(F32)) * g_ref[...]
    o_ref[...] = x_ref[...] + gate_ref[0] * y


def _mod_row_map(rows_per_mod_row, fixed_row):
    if fixed_row is not None:
        return lambda i: (fixed_row, 0, 0)
    return lambda i: (i // rows_per_mod_row, 0, 0)


def _normmod(x2, g, scale3, shift3, *, tiles_per_batch=None, fixed_row=None, bm=256):
    m, d = x2.shape
    row_map = _mod_row_map(tiles_per_batch, fixed_row)
    return pl.pallas_call(
        _normmod_kernel,
        grid=(m // bm,),
        in_specs=[
            pl.BlockSpec((bm, d), lambda i: (i, 0)),
            pl.BlockSpec((1, d), lambda i: (0, 0)),
            pl.BlockSpec((1, 1, d), row_map),
            pl.BlockSpec((1, 1, d), row_map),
        ],
        out_specs=pl.BlockSpec((bm, d), lambda i: (i, 0)),
        out_shape=jax.ShapeDtypeStruct((m, d), BF16),
        compiler_params=_params(("arbitrary",)),
        name="normmod",
    )(x2, g.reshape(1, d), scale3, shift3)


def _resid(x2, o2, g, gate3, *, tiles_per_batch=None, fixed_row=None, bm=256):
    m, d = x2.shape
    row_map = _mod_row_map(tiles_per_batch, fixed_row)
    return pl.pallas_call(
        _resid_kernel,
        grid=(m // bm,),
        in_specs=[
            pl.BlockSpec((bm, d), lambda i: (i, 0)),
            pl.BlockSpec((bm, d), lambda i: (i, 0)),
            pl.BlockSpec((1, d), lambda i: (0, 0)),
            pl.BlockSpec((1, 1, d), row_map),
        ],
        out_specs=pl.BlockSpec((bm, d), lambda i: (i, 0)),
        out_shape=jax.ShapeDtypeStruct((m, d), F32),
        compiler_params=_params(("arbitrary",)),
        name="resid",
    )(x2, o2, g.reshape(1, d), gate3)


def _mm_kernel(x_ref, w_ref, o_ref):
    acc = jnp.dot(x_ref[...], w_ref[...], preferred_element_type=F32)
    o_ref[...] = acc.astype(o_ref.dtype)


def _mm_qk_kernel(x_ref, w_ref, cos_ref, sin_ref, o_ref, *, rope, q_tiles, q_scale):
    acc = jnp.dot(x_ref[...], w_ref[...], preferred_element_type=F32)
    sc = jnp.where(pl.program_id(1) < q_tiles, q_scale, 1.0).astype(F32)
    bn = acc.shape[1]
    if rope:
        cos = cos_ref[...]
        sin = sin_ref[...]
        lane = lax.broadcasted_iota(jnp.int32, (1, HEAD_DIM), 1)
        first = (lane % (HEAD_DIM // 2)) < (HEAD_DIM // 4)
    for c in range(bn // HEAD_DIM):
        xc = acc[:, c * HEAD_DIM:(c + 1) * HEAD_DIM]
        if rope:
            partner = jnp.where(first,
                                pltpu.roll(xc, HEAD_DIM - HEAD_DIM // 4, 1),
                                pltpu.roll(xc, HEAD_DIM // 4, 1))
            xc = xc * cos + partner * sin
        o_ref[:, c * HEAD_DIM:(c + 1) * HEAD_DIM] = (xc * sc).astype(o_ref.dtype)


def _mm_glu_kernel(x_ref, w_ref, y_ref, z_ref, o_ref):
    acc = jnp.dot(x_ref[...], w_ref[...], preferred_element_type=F32)
    y = y_ref[...].astype(F32)
    o_ref[...] = (y * _sigmoid(acc) * _silu(z_ref[...].astype(F32))).astype(o_ref.dtype)


def _mm_tiles(m, n):
    return min(m, 1024), min(n, 1024)


def _matmul(x, w, out_dtype=BF16):
    m, k = x.shape
    n = w.shape[1]
    bm, bn = _mm_tiles(m, n)
    return pl.pallas_call(
        _mm_kernel,
        grid=(m // bm, n // bn),
        in_specs=[pl.BlockSpec((bm, k), lambda i, j: (i, 0)),
                  pl.BlockSpec((k, bn), lambda i, j: (0, j))],
        out_specs=pl.BlockSpec((bm, bn), lambda i, j: (i, j)),
        out_shape=jax.ShapeDtypeStruct((m, n), out_dtype),
        compiler_params=_params(("arbitrary", "arbitrary")),
        name="proj",
    )(x, w)


def _matmul_qk(x, w, cos_t, sin_t, *, rope):
    m, k = x.shape
    n = w.shape[1]
    bm, bn = _mm_tiles(m, n)
    tiles_per_seq = cos_t.shape[0] // bm if rope else 1
    tab_map = (lambda i, j: (i % tiles_per_seq, 0)) if rope else (lambda i, j: (0, 0))
    tab_rows = bm if rope else V7X_SUBLANES
    kern = functools.partial(_mm_qk_kernel, rope=rope, q_tiles=(n // 2) // bn,
                             q_scale=HEAD_DIM ** -0.5 * math.log2(math.e))
    return pl.pallas_call(
        kern,
        grid=(m // bm, n // bn),
        in_specs=[pl.BlockSpec((bm, k), lambda i, j: (i, 0)),
                  pl.BlockSpec((k, bn), lambda i, j: (0, j)),
                  pl.BlockSpec((tab_rows, HEAD_DIM), tab_map),
                  pl.BlockSpec((tab_rows, HEAD_DIM), tab_map)],
        out_specs=pl.BlockSpec((bm, bn), lambda i, j: (i, j)),
        out_shape=jax.ShapeDtypeStruct((m, n), BF16),
        compiler_params=_params(("arbitrary", "arbitrary")),
        name="proj_qk_rope" if rope else "proj_qk",
    )(x, w, cos_t, sin_t)


def _matmul_glu(y, w, p):
    m, k = y.shape
    n = w.shape[1]
    bm, bn = _mm_tiles(m, n)
    z_off = n // bn
    return pl.pallas_call(
        _mm_glu_kernel,
        grid=(m // bm, n // bn),
        in_specs=[pl.BlockSpec((bm, k), lambda i, j: (i, 0)),
                  pl.BlockSpec((k, bn), lambda i, j: (0, j)),
                  pl.BlockSpec((bm, bn), lambda i, j: (i, j)),
                  pl.BlockSpec((bm, bn), lambda i, j: (i, z_off + j))],
        out_specs=pl.BlockSpec((bm, bn), lambda i, j: (i, j)),
        out_shape=jax.ShapeDtypeStruct((m, n), BF16),
        compiler_params=_params(("arbitrary", "arbitrary")),
        name="proj_glu",
    )(y, w, y, p)


ATT_KEY_CHUNK = 512
ATT_LAG = 4


def _lane_block_reduce(op, x):
    return functools.reduce(op, [x[:, V7X_LANES * j:V7X_LANES * (j + 1)] for j in range(x.shape[1] // V7X_LANES)])


def _attn_kernel(lam_ref, g_ref, q_ref, gate_ref, *refs, n_seg, lam_init):
    o_ref, s_scr = refs[2 * n_seg:]
    segs = [(refs[2 * i], refs[2 * i + 1]) for i in range(n_seg)]
    lv = lam_ref[...]
    t1 = jnp.sum(lv[0:1] * lv[1:2], axis=1, keepdims=True)
    t2 = jnp.sum(lv[2:3] * lv[3:4], axis=1, keepdims=True)
    lam = jnp.exp(t1) - jnp.exp(t2) + lam_init

    ch = s_scr.shape[3]
    chunks = []
    for si, (k, _) in enumerate(segs):
        for start in range(0, k.shape[0], ch):
            chunks.append((si, start, min(ch, k.shape[0] - start), len(chunks)))
    n_main = segs[-1][0].shape[0] // ch
    n_head = len(chunks) - n_main

    q = [q_ref[:, :HEAD_DIM], q_ref[:, HEAD_DIM:]]
    mrun = [None, None]
    for si, start, size, c in chunks:
        k_ref = segs[si][0]
        for n in range(2):
            s = lax.dot_general(q[n], k_ref[start:start + size, HEAD_DIM * n:HEAD_DIM * (n + 1)],
                                _DN_LAST, preferred_element_type=F32)
            s_scr[n, c, :, :size] = s
            part = _lane_block_reduce(jnp.maximum, s)
            mrun[n] = part if mrun[n] is None else jnp.maximum(mrun[n], part)
    m = [jnp.max(mrun[n], axis=1, keepdims=True) for n in range(2)]

    tq = q_ref.shape[0]
    m_bits = [pltpu.bitcast(jnp.broadcast_to(m[n], (tq, V7X_LANES)), jnp.uint32) for n in range(2)]
    lrun = [None, None]
    acc = [None, None]
    hist = [[], []]
    for idx, (si, start, size, c) in enumerate(chunks):
        v = segs[si][1][start:start + size, :]
        for n in range(2):
            mb = m_bits[n]
            if idx >= ATT_LAG:
                mb = (mb.reshape(tq // V7X_SUBLANES, V7X_SUBLANES, V7X_LANES) | hist[n][idx - ATT_LAG][None])
                mb = mb.reshape(tq, V7X_LANES)
            mc = pltpu.bitcast(mb, F32)
            p = jnp.concatenate([jnp.exp2(s_scr[n, c, :, V7X_LANES * j:V7X_LANES * (j + 1)] - mc)
                                 for j in range(size // V7X_LANES)], axis=1)
            part = _lane_block_reduce(jnp.add, p)
            lrun[n] = part if lrun[n] is None else lrun[n] + part
            pv = jnp.dot(p.astype(BF16), v, preferred_element_type=F32)
            acc[n] = pv if acc[n] is None else acc[n] + pv
            tail = pltpu.bitcast(pv[tq - V7X_SUBLANES:, V_DIM - V7X_LANES:], jnp.uint32)
            hist[n].append((tail >> 16) >> 16)
    l1 = jnp.sum(lrun[0], axis=1, keepdims=True)
    l2 = jnp.sum(lrun[1], axis=1, keepdims=True)
    o = acc[0] * (1.0 / l1) - acc[1] * (lam / l2)
    o = _rms(o) * g_ref[...] * (1.0 - lam_init)
    o_ref[...] = (o * _silu(gate_ref[...].astype(F32))).astype(o_ref.dtype)


def _attention(lam_vecs, subln_g, q_arr, gate_arr, segs, *, batch, n_q, tq, lam_init):
    nq_tiles = n_q // tq
    n_seg = len(segs)
    in_specs = [
        pl.BlockSpec((4, HEAD_DIM), lambda b, h, i: (0, 0)),
        pl.BlockSpec((1, V_DIM), lambda b, h, i: (0, 0)),
        pl.BlockSpec((tq, V_DIM), lambda b, h, i: (b * nq_tiles + i, h)),
        pl.BlockSpec((tq, V_DIM), lambda b, h, i: (b * nq_tiles + i, HEADS + h)),
    ]
    args = [lam_vecs, subln_g.reshape(1, V_DIM), q_arr, gate_arr]
    for k_arr, v_arr, rows in segs:
        in_specs.append(pl.BlockSpec((rows, V_DIM), lambda b, h, i: (b, HEADS + h)))
        in_specs.append(pl.BlockSpec((rows, V_DIM), lambda b, h, i: (b, h)))
        args += [k_arr, v_arr]
    kern = functools.partial(_attn_kernel, n_seg=n_seg, lam_init=lam_init)
    ch = min(ATT_KEY_CHUNK, min(rows for _, _, rows in segs))
    n_chunks = sum(rows // ch for _, _, rows in segs)
    return pl.pallas_call(
        kern,
        grid=(batch, HEADS, nq_tiles),
        in_specs=in_specs,
        out_specs=pl.BlockSpec((tq, V_DIM), lambda b, h, i: (b * nq_tiles + i, h)),
        out_shape=jax.ShapeDtypeStruct((batch * n_q, HEADS * V_DIM), BF16),
        scratch_shapes=[pltpu.VMEM((2, n_chunks, tq, ch), F32)],
        compiler_params=_params(("arbitrary", "arbitrary", "arbitrary")),
        name="diff_attn",
    )(*args)


S5_BLOCK_GROUPS = V7X_LANES // S5_GROUP
S5_PASS_GROUPS = 4
S5_ROW_CHUNK = 32


def _shift_lanes_256(x, n, lane):
    a, b = x[:, :V7X_LANES], x[:, V7X_LANES:]
    zero = jnp.zeros_like(a)
    if n == 0:
        return x
    if n > 0:
        q, r = divmod(n, V7X_LANES)
        if r == 0:
            lo, hi = zero, a
        else:
            ra, rb = pltpu.roll(a, r, 1), pltpu.roll(b, r, 1)
            keep = lane >= r
            if q == 0:
                lo, hi = jnp.where(keep, ra, 0.0), jnp.where(keep, rb, ra)
            else:
                lo, hi = zero, jnp.where(keep, ra, 0.0)
    else:
        q, r = divmod(-n, V7X_LANES)
        if r == 0:
            lo, hi = b, zero
        else:
            ra, rb = pltpu.roll(a, V7X_LANES - r, 1), pltpu.roll(b, V7X_LANES - r, 1)
            keep = lane < V7X_LANES - r
            if q == 0:
                lo, hi = jnp.where(keep, ra, rb), jnp.where(keep, rb, 0.0)
            else:
                lo, hi = jnp.where(keep, rb, 0.0), zero
    return jnp.concatenate([lo, hi], axis=1)


def _transpose_lane_blocks(vs, lane):
    vs = list(vs)
    d = len(vs) // 2
    while d >= 1:
        low = (lane // (S5_GROUP * d)) % 2 == 0
        for i in range(len(vs)):
            if i & d == 0:
                x, y = vs[i], vs[i + d]
                vs[i] = jnp.where(low, x, pltpu.roll(y, S5_GROUP * d, 1))
                vs[i + d] = jnp.where(low, pltpu.roll(x, V7X_LANES - S5_GROUP * d, 1), y)
        d //= 2
    return vs


def _s5_state_rows(batch, n_ctx_chunks, n_lat_chunks):
    ctx_pitch = n_ctx_chunks + V7X_SUBLANES
    lat_pitch = n_lat_chunks + V7X_SUBLANES
    lat_base = batch * ctx_pitch
    return ctx_pitch, lat_pitch, lat_base, lat_base + batch * lat_pitch


def _gelu_tanh(y):
    inner = math.sqrt(2.0 / math.pi) * (y + 0.044715 * (y * y * y))
    return 0.5 * y * (1.0 + jnp.tanh(inner))


def _s5_kernel(uctx_ref, ulat_ref, d_ref, st_ref, rt_ref, kc_ref, bt_ref, coef_ref, o_ref,
               u32, u_scr, s_scr, y_scr, m_scr, *, batch, n_ctx_chunks, n_lat_chunks):
    ng = S5_BLOCK_GROUPS
    rc = S5_ROW_CHUNK
    ctx_rows = batch * n_ctx_chunks
    lat_rows = batch * n_lat_chunks
    rows = ctx_rows + lat_rows
    ctx_pitch, lat_pitch, lat_base = _s5_state_rows(batch, n_ctx_chunks, n_lat_chunks)[:3]
    lane = lax.broadcasted_iota(jnp.int32, (1, V7X_LANES), 1)

    u32[0:ctx_rows * CHUNK, :] = uctx_ref[...].astype(F32)
    u32[ctx_rows * CHUNK:, :] = ulat_ref[...].astype(F32)

    def gather(i, carry):
        r0 = pl.multiple_of(i * rc, rc)
        for h in range(2):
            vs = [u32[pl.ds(r0 * CHUNK + ng * h + k, rc, stride=CHUNK), :] for k in range(ng)]
            for g, x in enumerate(_transpose_lane_blocks(vs, lane)):
                u_scr[g, pl.ds(r0, rc), V7X_LANES * h:V7X_LANES * (h + 1)] = x.astype(BF16)
        return carry

    lax.fori_loop(0, rows // rc, gather, 0)

    for g in range(ng):
        kf = lax.dot_general(bt_ref[g, 0], kc_ref[g, 0], _DN_LAST, preferred_element_type=F32,
                             precision=lax.Precision.HIGHEST)
        kb = lax.dot_general(bt_ref[g, 1], kc_ref[g, 1], _DN_LAST, preferred_element_type=F32,
                             precision=lax.Precision.HIGHEST)
        for s in range(CHUNK):
            blk = (_shift_lanes_256(kf, S5_GROUP * s, lane)
                   + _shift_lanes_256(kb, -S5_GROUP * (CHUNK - 1 - s), lane))
            m_scr[g, S5_GROUP * s:S5_GROUP * (s + 1), :] = blk.astype(BF16)

    for p in range(ng // S5_PASS_GROUPS):
        g0 = p * S5_PASS_GROUPS
        for gi in range(S5_PASS_GROUPS):
            o1 = lax.dot_general(u_scr[g0 + gi], st_ref[g0 + gi], _DN_LAST, preferred_element_type=F32)
            for cb in range(4):
                col = o1[:, V7X_LANES * cb:V7X_LANES * (cb + 1)]
                for b in range(batch):
                    s_scr[gi, cb, ctx_pitch * b:ctx_pitch * b + n_ctx_chunks, :] = (
                        col[n_ctx_chunks * b:n_ctx_chunks * (b + 1)])
                    s_scr[gi, cb, lat_base + lat_pitch * b:lat_base + lat_pitch * b + n_lat_chunks, :] = (
                        col[ctx_rows + n_lat_chunks * b:ctx_rows + n_lat_chunks * (b + 1)])

        coefs = [[coef_ref[g0 + gi, i:i + 1, :] for i in range(4)] for gi in range(S5_PASS_GROUPS)]

        def make_step(base, n, pitch):
            def step(i, carry):
                out = []
                for gi in range(S5_PASS_GROUPS):
                    hf, hsf, hb, hsb = carry[4 * gi:4 * gi + 4]
                    rf = pl.ds(base + i, batch, stride=pitch)
                    rb = pl.ds(base + n - 1 - i, batch, stride=pitch)
                    sf, ssf = s_scr.at[gi, 0][rf, :], s_scr.at[gi, 1][rf, :]
                    sb, ssb = s_scr.at[gi, 2][rb, :], s_scr.at[gi, 3][rb, :]
                    s_scr.at[gi, 0][rf, :] = hf
                    s_scr.at[gi, 2][rb, :] = hb
                    caf, cbf, cab, cbb = coefs[gi]
                    out += [caf * hf + cbf * hsf + sf, caf * hsf - cbf * hf + ssf,
                            cab * hb + cbb * hsb + sb, cab * hsb - cbb * hb + ssb]
                return tuple(out)
            return step

        carry = tuple(jnp.zeros((batch, V7X_LANES), F32) for _ in range(4 * S5_PASS_GROUPS))
        carry = lax.fori_loop(0, n_ctx_chunks, make_step(0, n_ctx_chunks, ctx_pitch), carry)
        lax.fori_loop(0, n_lat_chunks, make_step(lat_base, n_lat_chunks, lat_pitch), carry)

        def lat_states(gi, cb):
            return jnp.concatenate([s_scr[gi, cb, lat_base + lat_pitch * b:lat_base + lat_pitch * b + n_lat_chunks, :]
                                    for b in range(batch)], axis=0)

        for gi in range(S5_PASS_GROUPS):
            g = g0 + gi
            hin = jnp.concatenate([lat_states(gi, 0), lat_states(gi, 2)], axis=1)
            y = jnp.dot(u_scr[g, ctx_rows:, :], m_scr[g], preferred_element_type=F32)
            y = y + lax.dot_general(hin.astype(BF16), rt_ref[g], _DN_LAST, preferred_element_type=F32)
            y_scr[g] = y

    d = d_ref[...]

    def scatter(i, carry):
        r0 = pl.multiple_of(i * rc, rc)
        for h in range(2):
            ys = [y_scr[g, pl.ds(r0, rc), V7X_LANES * h:V7X_LANES * (h + 1)] for g in range(ng)]
            for k, x in enumerate(_transpose_lane_blocks(ys, lane)):
                rows_t = pl.ds((ctx_rows + r0) * CHUNK + ng * h + k, rc, stride=CHUNK)
                u32[rows_t, :] = _gelu_tanh(x + d * u32[rows_t, :])
        return carry

    lax.fori_loop(0, lat_rows // rc, scatter, 0)
    o_ref[...] = u32[ctx_rows * CHUNK:, :].astype(o_ref.dtype)


def _s5_mix(u_ctx, p_lat, d_skip, st, rt, kc, bt, coef, *, batch, n_ctx, seq):
    ng = S5_BLOCK_GROUPS
    width = u_ctx.shape[1]
    n_ctx_chunks, n_lat_chunks = n_ctx // CHUNK, seq // CHUNK
    rows = batch * (n_ctx_chunks + n_lat_chunks)
    lat_rows = batch * n_lat_chunks
    kern = functools.partial(_s5_kernel, batch=batch, n_ctx_chunks=n_ctx_chunks, n_lat_chunks=n_lat_chunks)
    one = pl.Buffered(1)
    return pl.pallas_call(
        kern,
        grid=(width // V7X_LANES,),
        in_specs=[
            pl.BlockSpec((batch * n_ctx, V7X_LANES), lambda i: (0, i)),
            pl.BlockSpec((batch * seq, V7X_LANES), lambda i: (0, i), pipeline_mode=one),
            pl.BlockSpec((1, V7X_LANES), lambda i: (0, i)),
            pl.BlockSpec((ng,) + st.shape[1:], lambda i: (i, 0, 0)),
            pl.BlockSpec((ng,) + rt.shape[1:], lambda i: (i, 0, 0)),
            pl.BlockSpec((ng,) + kc.shape[1:], lambda i: (i, 0, 0, 0)),
            pl.BlockSpec((ng,) + bt.shape[1:], lambda i: (i, 0, 0, 0)),
            pl.BlockSpec((ng,) + coef.shape[1:], lambda i: (i, 0, 0)),
        ],
        out_specs=pl.BlockSpec((batch * seq, V7X_LANES), lambda i: (0, i)),
        out_shape=jax.ShapeDtypeStruct((batch * seq, width), BF16),
        scratch_shapes=[
            pltpu.VMEM((rows * CHUNK, V7X_LANES), F32),
            pltpu.VMEM((ng, rows, CHUNK_W), BF16),
            pltpu.VMEM((S5_PASS_GROUPS, 4, _s5_state_rows(batch, n_ctx_chunks, n_lat_chunks)[3], V7X_LANES), F32),
            pltpu.VMEM((ng, lat_rows, CHUNK_W), F32),
            pltpu.VMEM((ng, CHUNK_W, CHUNK_W), BF16),
        ],
        compiler_params=_params(("arbitrary",), vmem=V7X_VMEM_BYTES - 6 * 1024 * 1024),
        name="s5_mix",
    )(u_ctx, p_lat, d_skip.reshape(1, width), st, rt, kc, bt, coef)


def _s5_tables(A_re, A_im, log_dt, B_re, B_im, C_re, C_im):
    g = A_re.shape[1]
    dt = jnp.exp(log_dt)[..., None]
    mag = jnp.exp(A_re * dt)
    a_re = mag * jnp.cos(A_im * dt)
    a_im = mag * jnp.sin(A_im * dt)
    den = A_re * A_re + A_im * A_im
    f_re = ((a_re - 1.0) * A_re + a_im * A_im) / den
    f_im = (a_im * A_re - (a_re - 1.0) * A_im) / den
    bb_re = f_re[..., None] * B_re - f_im[..., None] * B_im
    bb_im = f_re[..., None] * B_im + f_im[..., None] * B_re

    pr, pi = [jnp.ones_like(a_re)], [jnp.zeros_like(a_im)]
    for _ in range(CHUNK):
        pr, pi = pr + [pr[-1] * a_re - pi[-1] * a_im], pi + [pr[-1] * a_im + pi[-1] * a_re]
    pw_re, pw_im = jnp.stack(pr, axis=2), jnp.stack(pi, axis=2)

    ca_re = C_re[:, :, None] * pw_re[:, :, :, None, :] - C_im[:, :, None] * pw_im[:, :, :, None, :]
    ca_im = C_re[:, :, None] * pw_im[:, :, :, None, :] + C_im[:, :, None] * pw_re[:, :, :, None, :]
    ca = jnp.concatenate([ca_re, -ca_im], axis=-1)

    kc = jnp.stack([ca[0][:, :CHUNK], ca[1][:, CHUNK - 1::-1]], axis=1).reshape(g, 2, CHUNK_W, 2 * S5_STATE)
    rt = jnp.concatenate([ca[0][:, 1:].reshape(g, CHUNK_W, 2 * S5_STATE),
                          ca[1][:, :0:-1].reshape(g, CHUNK_W, 2 * S5_STATE)], axis=-1).astype(BF16)
    bt = jnp.concatenate([bb_re.transpose(0, 1, 3, 2), bb_im.transpose(0, 1, 3, 2)], axis=-1)
    bt = bt.transpose(1, 0, 2, 3)

    def state_rows(d, powers):
        qr = pw_re[d][:, powers].transpose(0, 2, 1)[..., None]
        qi = pw_im[d][:, powers].transpose(0, 2, 1)[..., None]
        br, bi = bb_re[d][:, :, None, :], bb_im[d][:, :, None, :]
        s_re = (qr * br - qi * bi).reshape(g, S5_STATE, CHUNK_W)
        s_im = (qr * bi + qi * br).reshape(g, S5_STATE, CHUNK_W)
        return jnp.concatenate([s_re, s_im, s_im, s_re], axis=1)

    k_idx = jnp.arange(CHUNK)
    st = jnp.concatenate([state_rows(0, CHUNK - 1 - k_idx), state_rows(1, k_idx)], axis=1).astype(BF16)

    ar, ai = pw_re[:, :, CHUNK], pw_im[:, :, CHUNK]
    c_a = jnp.concatenate([ar, ar], axis=-1)
    c_b = jnp.concatenate([-ai, ai], axis=-1)
    coef = jnp.stack([c_a[0], c_b[0], c_a[1], c_b[1]], axis=1)
    return st, rt, kc, bt, coef


def _rope_tables(n_tokens):
    rows = n_tokens // GRID_W
    row = jnp.repeat(jnp.arange(rows), GRID_W).astype(F32)
    col = jnp.tile(jnp.arange(GRID_W), rows).astype(F32)
    n_freq = HEAD_DIM // 4
    inv_freq = ROPE_BASE ** (-jnp.arange(n_freq, dtype=F32) / n_freq)
    ar = row[:, None] * inv_freq
    ac = col[:, None] * inv_freq
    cos_t = jnp.concatenate([jnp.cos(ar), jnp.cos(ar), jnp.cos(ac), jnp.cos(ac)], axis=1)
    sin_t = jnp.concatenate([-jnp.sin(ar), jnp.sin(ar), -jnp.sin(ac), jnp.sin(ac)], axis=1)
    return cos_t, sin_t


def kernel(x, c, ctx, c_ctx, ada_w, ada_b, norm_pre, norm_post, attn_w_in, attn_w_out, attn_lam, attn_subln, s5_w_in, s5_A_re, s5_A_im, s5_log_dt, s5_B_re, s5_B_im, s5_C_re, s5_C_im, s5_D, s5_w_glu, s5_w_out):
    bsz, seq, d = x.shape
    n_ctx = ctx.shape[1]
    depth = ada_w.shape[0]
    assert d == D_MODEL and depth == 2 and bsz + 1 <= MOD_ROWS
    ctx_row = bsz

    cc = jnp.concatenate([c, c_ctx[None, :], jnp.zeros((MOD_ROWS - bsz - 1, d), F32)], axis=0)
    mods = _modulation(cc, ada_w, ada_b)
    mods = mods.reshape(depth, MOD_ROWS, 3, 1, d).transpose(0, 2, 1, 3, 4)

    x_lat = x.reshape(bsz * seq, d)
    x_ctx = ctx.reshape(bsz * n_ctx, d)
    norm_bm = 256
    lat_tiles = seq // norm_bm

    shift, scale, gate = mods[0, 0], mods[0, 1], mods[0, 2]
    h_lat = _normmod(x_lat, norm_pre[0], scale, shift, tiles_per_batch=lat_tiles, bm=norm_bm)
    h_ctx = _normmod(x_ctx, norm_pre[0], scale, shift, fixed_row=ctx_row, bm=norm_bm)

    w_in = attn_w_in[0].astype(BF16)
    w_qk, w_vg = w_in[:, :2 * QK_W], w_in[:, 2 * QK_W:]
    cos_t, sin_t = _rope_tables(seq)
    qk_lat = _matmul_qk(h_lat, w_qk, cos_t, sin_t, rope=True)
    vg_lat = _matmul(h_lat, w_vg)
    dummy = jnp.zeros((V7X_SUBLANES, HEAD_DIM), F32)
    qk_ctx = _matmul_qk(h_ctx, w_qk, dummy, dummy, rope=False)
    vg_ctx = _matmul(h_ctx, w_vg)

    lam_init = 0.8 - 0.6 * math.exp(-0.3 * 0)
    a_lat = _attention(attn_lam[0], attn_subln[0], qk_lat, vg_lat,
                       [(qk_ctx, vg_ctx, n_ctx), (qk_lat, vg_lat, seq)],
                       batch=bsz, n_q=seq, tq=512, lam_init=lam_init)
    a_ctx = _attention(attn_lam[0], attn_subln[0], qk_ctx, vg_ctx,
                       [(qk_ctx, vg_ctx, n_ctx)],
                       batch=bsz, n_q=n_ctx, tq=n_ctx, lam_init=lam_init)
    w_out = attn_w_out[0].astype(BF16)
    o_lat = _matmul(a_lat, w_out)
    o_ctx = _matmul(a_ctx, w_out)
    x_lat = _resid(x_lat, o_lat, norm_post[0], gate, tiles_per_batch=lat_tiles, bm=norm_bm)
    x_ctx = _resid(x_ctx, o_ctx, norm_post[0], gate, fixed_row=ctx_row, bm=norm_bm)

    shift, scale, gate = mods[1, 0], mods[1, 1], mods[1, 2]
    h_lat = _normmod(x_lat, norm_pre[1], scale, shift, tiles_per_batch=lat_tiles, bm=norm_bm)
    h_ctx = _normmod(x_ctx, norm_pre[1], scale, shift, fixed_row=ctx_row, bm=norm_bm)
    w_in = s5_w_in[0].astype(BF16)
    p_lat = _matmul(h_lat, w_in)
    u_ctx = _matmul(h_ctx, w_in[:, :d])

    st, rt, kc, bt, coef = _s5_tables(s5_A_re[0], s5_A_im[0], s5_log_dt[0], s5_B_re[0], s5_B_im[0],
                                      s5_C_re[0], s5_C_im[0])
    yact = _s5_mix(u_ctx, p_lat, s5_D[0], st, rt, kc, bt, coef, batch=bsz, n_ctx=n_ctx, seq=seq)
    gl = _matmul_glu(yact, s5_w_glu[0].astype(BF16), p_lat)
    o_lat = _matmul(gl, s5_w_out[0].astype(BF16))
    x_lat = _resid(x_lat, o_lat, norm_post[1], gate, tiles_per_batch=lat_tiles, bm=norm_bm)
    return x_lat.reshape(bsz, seq, d)
```

```python
import functools
import math

import jax
import jax.numpy as jnp
from jax import lax
from jax.experimental import pallas as pl
from jax.experimental.pallas import tpu as pltpu

F32 = jnp.float32
BF16 = jnp.bfloat16

V7X_LANES = 128
V7X_SUBLANES = 8
V7X_VMEM_BYTES = 64 * 1024 * 1024
VMEM_LIMIT = V7X_VMEM_BYTES - 8 * 1024 * 1024

D_MODEL = 4096
GRID_W = 64
EPS = 1e-6
HEADS = 16
HEAD_DIM = 128
V_DIM = 2 * HEAD_DIM
QK_W = HEADS * 2 * HEAD_DIM
ROPE_BASE = 10000.0
S5_GROUP = 16
S5_GROUPS = D_MODEL // S5_GROUP
S5_STATE = 64
CHUNK = 16
CHUNK_W = CHUNK * S5_GROUP
MOD_ROWS = 8
_DN_LAST = (((1,), (1,)), ((), ()))


def _params(sem, vmem=VMEM_LIMIT):
    return pltpu.CompilerParams(dimension_semantics=sem, vmem_limit_bytes=vmem)


def _sigmoid(x):
    return 1.0 / (1.0 + jnp.exp(-x))


def _silu(x):
    return x * _sigmoid(x)


def _mod_kernel(c_ref, w_ref, b_ref, o_ref):
    a = _silu(c_ref[...])
    acc = jnp.dot(a, w_ref[0], preferred_element_type=F32, precision=lax.Precision.HIGHEST)
    o_ref[0] = acc + b_ref[0]


def _modulation(cc, ada_w, ada_b):
    depth, d, n = ada_w.shape
    bn = 1024
    return pl.pallas_call(
        _mod_kernel,
        grid=(depth, n // bn),
        in_specs=[
            pl.BlockSpec((MOD_ROWS, d), lambda i, j: (0, 0)),
            pl.BlockSpec((1, d, bn), lambda i, j: (i, 0, j)),
            pl.BlockSpec((1, 1, bn), lambda i, j: (i, 0, j)),
        ],
        out_specs=pl.BlockSpec((1, MOD_ROWS, bn), lambda i, j: (i, 0, j)),
        out_shape=jax.ShapeDtypeStruct((depth, MOD_ROWS, n), F32),
        compiler_params=_params(("arbitrary", "arbitrary")),
        name="modulation",
    )(cc, ada_w, ada_b.reshape(depth, 1, n))


def _rms(x):
    return x * lax.rsqrt(jnp.mean(x * x, axis=-1, keepdims=True) + EPS)


def _normmod_kernel(x_ref, g_ref, scale_ref, shift_ref, o_ref):
    y = _rms(x_ref[...]) * g_ref[...]
    o_ref[...] = (y * (1.0 + scale_ref[0]) + shift_ref[0]).astype(o_ref.dtype)


def _resid_kernel(x_ref, o_in_ref, g_ref, gate_ref, o_ref):
    y = _rms(o_in_ref[...].astype(F32)) * g_ref[...]
    o_ref[...] = x_ref[...] + gate_ref[0] * y


def _mod_row_map(rows_per_mod_row, fixed_row):
    if fixed_row is not None:
        return lambda i: (fixed_row, 0, 0)
    return lambda i: (i // rows_per_mod_row, 0, 0)


def _normmod(x2, g, scale3, shift3, *, tiles_per_batch=None, fixed_row=None, bm=256):
    m, d = x2.shape
    row_map = _mod_row_map(tiles_per_batch, fixed_row)
    return pl.pallas_call(
        _normmod_kernel,
        grid=(m // bm,),
        in_specs=[
            pl.BlockSpec((bm, d), lambda i: (i, 0)),
            pl.BlockSpec((1, d), lambda i: (0, 0)),
            pl.BlockSpec((1, 1, d), row_map),
            pl.BlockSpec((1, 1, d), row_map),
        ],
        out_specs=pl.BlockSpec((bm, d), lambda i: (i, 0)),
        out_shape=jax.ShapeDtypeStruct((m, d), BF16),
        compiler_params=_params(("arbitrary",)),
        name="normmod",
    )(x2, g.reshape(1, d), scale3, shift3)


def _resid(x2, o2, g, gate3, *, tiles_per_batch=None, fixed_row=None, bm=256):
    m, d = x2.shape
    row_map = _mod_row_map(tiles_per_batch, fixed_row)
    return pl.pallas_call(
        _resid_kernel,
        grid=(m // bm,),
        in_specs=[
            pl.BlockSpec((bm, d), lambda i: (i, 0)),
            pl.BlockSpec((bm, d), lambda i: (i, 0)),
            pl.BlockSpec((1, d), lambda i: (0, 0)),
            pl.BlockSpec((1, 1, d), row_map),
        ],
        out_specs=pl.BlockSpec((bm, d), lambda i: (i, 0)),
        out_shape=jax.ShapeDtypeStruct((m, d), F32),
        compiler_params=_params(("arbitrary",)),
        name="resid",
    )(x2, o2, g.reshape(1, d), gate3)


def _mm_kernel(x_ref, w_ref, o_ref):
    acc = jnp.dot(x_ref[...], w_ref[...], preferred_element_type=F32)
    o_ref[...] = acc.astype(o_ref.dtype)


def _mm_qk_kernel(x_ref, w_ref, cos_ref, sin_ref, o_ref, *, rope, q_tiles, q_scale):
    acc = jnp.dot(x_ref[...], w_ref[...], preferred_element_type=F32)
    sc = jnp.where(pl.program_id(1) < q_tiles, q_scale, 1.0).astype(F32)
    bn = acc.shape[1]
    if rope:
        cos = cos_ref[...]
        sin = sin_ref[...]
        lane = lax.broadcasted_iota(jnp.int32, (1, HEAD_DIM), 1)
        first = (lane % (HEAD_DIM // 2)) < (HEAD_DIM // 4)
    for c in range(bn // HEAD_DIM):
        xc = acc[:, c * HEAD_DIM:(c + 1) * HEAD_DIM]
        if rope:
            partner = jnp.where(first,
                                pltpu.roll(xc, HEAD_DIM - HEAD_DIM // 4, 1),
                                pltpu.roll(xc, HEAD_DIM // 4, 1))
            xc = xc * cos + partner * sin
        o_ref[:, c * HEAD_DIM:(c + 1) * HEAD_DIM] = (xc * sc).astype(o_ref.dtype)


def _mm_glu_kernel(x_ref, w_ref, y_ref, z_ref, o_ref):
    acc = jnp.dot(x_ref[...], w_ref[...], preferred_element_type=F32)
    y = y_ref[...].astype(F32)
    o_ref[...] = (y * _sigmoid(acc) * _silu(z_ref[...].astype(F32))).astype(o_ref.dtype)


def _mm_tiles(m, n):
    return min(m, 1024), min(n, 1024)


def _matmul(x, w, out_dtype=BF16):
    m, k = x.shape
    n = w.shape[1]
    bm, bn = _mm_tiles(m, n)
    return pl.pallas_call(
        _mm_kernel,
        grid=(m // bm, n // bn),
        in_specs=[pl.BlockSpec((bm, k), lambda i, j: (i, 0)),
                  pl.BlockSpec((k, bn), lambda i, j: (0, j))],
        out_specs=pl.BlockSpec((bm, bn), lambda i, j: (i, j)),
        out_shape=jax.ShapeDtypeStruct((m, n), out_dtype),
        compiler_params=_params(("arbitrary", "arbitrary")),
        name="proj",
    )(x, w)


def _matmul_qk(x, w, cos_t, sin_t, *, rope):
    m, k = x.shape
    n = w.shape[1]
    bm, bn = _mm_tiles(m, n)
    tiles_per_seq = cos_t.shape[0] // bm if rope else 1
    tab_map = (lambda i, j: (i % tiles_per_seq, 0)) if rope else (lambda i, j: (0, 0))
    tab_rows = bm if rope else V7X_SUBLANES
    kern = functools.partial(_mm_qk_kernel, rope=rope, q_tiles=(n // 2) // bn,
                             q_scale=HEAD_DIM ** -0.5 * math.log2(math.e))
    return pl.pallas_call(
        kern,
        grid=(m // bm, n // bn),
        in_specs=[pl.BlockSpec((bm, k), lambda i, j: (i, 0)),
                  pl.BlockSpec((k, bn), lambda i, j: (0, j)),
                  pl.BlockSpec((tab_rows, HEAD_DIM), tab_map),
                  pl.BlockSpec((tab_rows, HEAD_DIM), tab_map)],
        out_specs=pl.BlockSpec((bm, bn), lambda i, j: (i, j)),
        out_shape=jax.ShapeDtypeStruct((m, n), BF16),
        compiler_params=_params(("arbitrary", "arbitrary")),
        name="proj_qk_rope" if rope else "proj_qk",
    )(x, w, cos_t, sin_t)


def _matmul_glu(y, w, p):
    m, k = y.shape
    n = w.shape[1]
    bm, bn = _mm_tiles(m, n)
    z_off = n // bn
    return pl.pallas_call(
        _mm_glu_kernel,
        grid=(m // bm, n // bn),
        in_specs=[pl.BlockSpec((bm, k), lambda i, j: (i, 0)),
                  pl.BlockSpec((k, bn), lambda i, j: (0, j)),
                  pl.BlockSpec((bm, bn), lambda i, j: (i, j)),
                  pl.BlockSpec((bm, bn), lambda i, j: (i, z_off + j))],
        out_specs=pl.BlockSpec((bm, bn), lambda i, j: (i, j)),
        out_shape=jax.ShapeDtypeStruct((m, n), BF16),
        compiler_params=_params(("arbitrary", "arbitrary")),
        name="proj_glu",
    )(y, w, y, p)


ATT_KEY_CHUNK = 256
ATT_LAG = 4


def _lane_block_reduce(op, x):
    return functools.reduce(op, [x[:, V7X_LANES * j:V7X_LANES * (j + 1)] for j in range(x.shape[1] // V7X_LANES)])


def _attn_kernel(lam_ref, g_ref, q_ref, gate_ref, *refs, n_seg, lam_init):
    o_ref, s_scr = refs[2 * n_seg:]
    segs = [(refs[2 * i], refs[2 * i + 1]) for i in range(n_seg)]
    lv = lam_ref[...]
    t1 = jnp.sum(lv[0:1] * lv[1:2], axis=1, keepdims=True)
    t2 = jnp.sum(lv[2:3] * lv[3:4], axis=1, keepdims=True)
    lam = jnp.exp(t1) - jnp.exp(t2) + lam_init

    ch = s_scr.shape[3]
    chunks = []
    for si, (k, _) in enumerate(segs):
        for start in range(0, k.shape[0], ch):
            chunks.append((si, start, min(ch, k.shape[0] - start), len(chunks)))
    n_main = segs[-1][0].shape[0] // ch
    n_head = len(chunks) - n_main

    q = [q_ref[:, :HEAD_DIM], q_ref[:, HEAD_DIM:]]
    mrun = [None, None]
    for si, start, size, c in chunks:
        k_ref = segs[si][0]
        for n in range(2):
            s = lax.dot_general(q[n], k_ref[start:start + size, HEAD_DIM * n:HEAD_DIM * (n + 1)],
                                _DN_LAST, preferred_element_type=F32)
            s_scr[n, c, :, :size] = s
            part = _lane_block_reduce(jnp.maximum, s)
            mrun[n] = part if mrun[n] is None else jnp.maximum(mrun[n], part)
    m = [jnp.max(mrun[n], axis=1, keepdims=True) for n in range(2)]

    tq = q_ref.shape[0]
    m_bits = [pltpu.bitcast(jnp.broadcast_to(m[n], (tq, V7X_LANES)), jnp.uint32) for n in range(2)]
    lrun = [None, None]
    acc = [None, None]
    hist = [[], []]
    for idx, (si, start, size, c) in enumerate(chunks):
        v = segs[si][1][start:start + size, :]
        for n in range(2):
            mb = m_bits[n]
            if idx >= ATT_LAG:
                mb = (mb.reshape(tq // V7X_SUBLANES, V7X_SUBLANES, V7X_LANES) | hist[n][idx - ATT_LAG][None])
                mb = mb.reshape(tq, V7X_LANES)
            mc = pltpu.bitcast(mb, F32)
            p = jnp.concatenate([jnp.exp2(s_scr[n, c, :, V7X_LANES * j:V7X_LANES * (j + 1)] - mc)
                                 for j in range(size // V7X_LANES)], axis=1)
            part = _lane_block_reduce(jnp.add, p)
            lrun[n] = part if lrun[n] is None else lrun[n] + part
            pv = jnp.dot(p.astype(BF16), v, preferred_element_type=F32)
            acc[n] = pv if acc[n] is None else acc[n] + pv
            tail = pltpu.bitcast(pv[tq - V7X_SUBLANES:, V_DIM - V7X_LANES:], jnp.uint32)
            hist[n].append((tail >> 16) >> 16)
    l1 = jnp.sum(lrun[0], axis=1, keepdims=True)
    l2 = jnp.sum(lrun[1], axis=1, keepdims=True)
    o = acc[0] * (1.0 / l1) - acc[1] * (lam / l2)
    o = _rms(o) * g_ref[...] * (1.0 - lam_init)
    o_ref[...] = (o * _silu(gate_ref[...].astype(F32))).astype(o_ref.dtype)


def _attention(lam_vecs, subln_g, q_arr, gate_arr, segs, *, batch, n_q, tq, lam_init):
    nq_tiles = n_q // tq
    n_seg = len(segs)
    in_specs = [
        pl.BlockSpec((4, HEAD_DIM), lambda b, h, i: (0, 0)),
        pl.BlockSpec((1, V_DIM), lambda b, h, i: (0, 0)),
        pl.BlockSpec((tq, V_DIM), lambda b, h, i: (b * nq_tiles + i, h)),
        pl.BlockSpec((tq, V_DIM), lambda b, h, i: (b * nq_tiles + i, HEADS + h)),
    ]
    args = [lam_vecs, subln_g.reshape(1, V_DIM), q_arr, gate_arr]
    for k_arr, v_arr, rows in segs:
        in_specs.append(pl.BlockSpec((rows, V_DIM), lambda b, h, i: (b, HEADS + h)))
        in_specs.append(pl.BlockSpec((rows, V_DIM), lambda b, h, i: (b, h)))
        args += [k_arr, v_arr]
    kern = functools.partial(_attn_kernel, n_seg=n_seg, lam_init=lam_init)
    ch = min(ATT_KEY_CHUNK, max(rows for _, _, rows in segs))
    n_chunks = sum(pl.cdiv(rows, ch) for _, _, rows in segs)
    return pl.pallas_call(
        kern,
        grid=(batch, HEADS, nq_tiles),
        in_specs=in_specs,
        out_specs=pl.BlockSpec((tq, V_DIM), lambda b, h, i: (b * nq_tiles + i, h)),
        out_shape=jax.ShapeDtypeStruct((batch * n_q, HEADS * V_DIM), BF16),
        scratch_shapes=[pltpu.VMEM((2, n_chunks, tq, ch), F32)],
        compiler_params=_params(("arbitrary", "arbitrary", "arbitrary")),
        name="diff_attn",
    )(*args)


S5_BLOCK_GROUPS = V7X_LANES // S5_GROUP
S5_PASS_GROUPS = 4
S5_ROW_CHUNK = 32


def _shift_lanes_256(x, n, lane):
    a, b = x[:, :V7X_LANES], x[:, V7X_LANES:]
    zero = jnp.zeros_like(a)
    if n == 0:
        return x
    if n > 0:
        q, r = divmod(n, V7X_LANES)
        if r == 0:
            lo, hi = zero, a
        else:
            ra, rb = pltpu.roll(a, r, 1), pltpu.roll(b, r, 1)
            keep = lane >= r
            if q == 0:
                lo, hi = jnp.where(keep, ra, 0.0), jnp.where(keep, rb, ra)
            else:
                lo, hi = zero, jnp.where(keep, ra, 0.0)
    else:
        q, r = divmod(-n, V7X_LANES)
        if r == 0:
            lo, hi = b, zero
        else:
            ra, rb = pltpu.roll(a, V7X_LANES - r, 1), pltpu.roll(b, V7X_LANES - r, 1)
            keep = lane < V7X_LANES - r
            if q == 0:
                lo, hi = jnp.where(keep, ra, rb), jnp.where(keep, rb, 0.0)
            else:
                lo, hi = jnp.where(keep, rb, 0.0), zero
    return jnp.concatenate([lo, hi], axis=1)


def _token_slot(t):
    return (t % 2) * (CHUNK // 2) + t // 2


def _transpose_lane_blocks(vs, lane):
    vs = list(vs)
    d = len(vs) // 2
    while d >= 1:
        low = (lane // (S5_GROUP * d)) % 2 == 0
        for i in range(len(vs)):
            if i & d == 0:
                x, y = vs[i], vs[i + d]
                vs[i] = jnp.where(low, x, pltpu.roll(y, S5_GROUP * d, 1))
                vs[i + d] = jnp.where(low, pltpu.roll(x, V7X_LANES - S5_GROUP * d, 1), y)
        d //= 2
    return vs


def _s5_state_rows(batch, n_ctx_chunks, n_lat_chunks):
    ctx_pitch = n_ctx_chunks + V7X_SUBLANES
    lat_pitch = n_lat_chunks + V7X_SUBLANES
    lat_base = batch * ctx_pitch
    return ctx_pitch, lat_pitch, lat_base, lat_base + batch * lat_pitch


def _gelu_tanh(y):
    inner = math.sqrt(2.0 / math.pi) * (y + 0.044715 * (y * y * y))
    return 0.5 * y * (1.0 + jnp.tanh(inner))


def _s5_kernel(uctx_ref, ulat_ref, dtab_ref, st_ref, rt_ref, kc_ref, bt_ref, coef_ref, pm_ref, o_ref,
               w_scr, u_scr, s_scr, y_scr, m_scr, mt_scr, *, batch, n_ctx_chunks, n_lat_chunks):
    ng = S5_BLOCK_GROUPS
    rc = S5_ROW_CHUNK
    pairs = CHUNK // 2
    ctx_rows = batch * n_ctx_chunks
    lat_rows = batch * n_lat_chunks
    ctx_pitch, lat_pitch, lat_base = _s5_state_rows(batch, n_ctx_chunks, n_lat_chunks)[:3]
    lane = lax.broadcasted_iota(jnp.int32, (1, V7X_LANES), 1)

    w_scr[0:ctx_rows * pairs, :] = pltpu.bitcast(uctx_ref[...], jnp.uint32)
    w_scr[ctx_rows * pairs:, :] = pltpu.bitcast(ulat_ref[...], jnp.uint32)

    def gather(i, carry):
        r0 = pl.multiple_of(i * rc, rc)
        vs = [w_scr[pl.ds(r0 * pairs + k, rc, stride=pairs), :] for k in range(pairs)]
        for g, x in enumerate(_transpose_lane_blocks(vs, lane)):
            for par, bits in enumerate((x << 16, x & jnp.uint32(0xFFFF0000))):
                tok = lax.bitcast_convert_type(bits, F32)
                u_scr[g, pl.ds(r0, rc), V7X_LANES * par:V7X_LANES * (par + 1)] = tok.astype(BF16)
        return carry

    lax.fori_loop(0, (ctx_rows + lat_rows) // rc, gather, 0, unroll=2)

    for g in range(ng):
        kf = lax.dot_general(bt_ref[g, 0], kc_ref[g, 0], _DN_LAST, preferred_element_type=F32,
                             precision=lax.Precision.HIGHEST)
        kb = lax.dot_general(bt_ref[g, 1], kc_ref[g, 1], _DN_LAST, preferred_element_type=F32,
                             precision=lax.Precision.HIGHEST)
        for s in range(CHUNK):
            blk = (_shift_lanes_256(kf, S5_GROUP * s, lane)
                   + _shift_lanes_256(kb, -S5_GROUP * (CHUNK - 1 - s), lane))
            pos = _token_slot(s)
            mt_scr[S5_GROUP * pos:S5_GROUP * (pos + 1), :] = blk.astype(BF16)
        m_scr[g] = jnp.dot(mt_scr[...], pm_ref[...], preferred_element_type=F32).astype(BF16)

    for p in range(ng // S5_PASS_GROUPS):
        g0 = p * S5_PASS_GROUPS
        for gi in range(S5_PASS_GROUPS):
            o1 = lax.dot_general(u_scr[g0 + gi], st_ref[g0 + gi], _DN_LAST, preferred_element_type=F32)
            for cb in range(4):
                col = o1[:, V7X_LANES * cb:V7X_LANES * (cb + 1)]
                for b in range(batch):
                    s_scr[gi, cb, ctx_pitch * b:ctx_pitch * b + n_ctx_chunks, :] = (
                        col[n_ctx_chunks * b:n_ctx_chunks * (b + 1)])
                    s_scr[gi, cb, lat_base + lat_pitch * b:lat_base + lat_pitch * b + n_lat_chunks, :] = (
                        col[ctx_rows + n_lat_chunks * b:ctx_rows + n_lat_chunks * (b + 1)])

        coefs = [[coef_ref[g0 + gi, i:i + 1, :] for i in range(4)] for gi in range(S5_PASS_GROUPS)]

        def make_step(base, n, pitch):
            def step(i, carry):
                out = []
                for gi in range(S5_PASS_GROUPS):
                    hf, hsf, hb, hsb = carry[4 * gi:4 * gi + 4]
                    rf = pl.ds(base + i, batch, stride=pitch)
                    rb = pl.ds(base + n - 1 - i, batch, stride=pitch)
                    sf, ssf = s_scr.at[gi, 0][rf, :], s_scr.at[gi, 1][rf, :]
                    sb, ssb = s_scr.at[gi, 2][rb, :], s_scr.at[gi, 3][rb, :]
                    s_scr.at[gi, 0][rf, :] = hf
                    s_scr.at[gi, 2][rb, :] = hb
                    caf, cbf, cab, cbb = coefs[gi]
                    out += [caf * hf + cbf * hsf + sf, caf * hsf - cbf * hf + ssf,
                            cab * hb + cbb * hsb + sb, cab * hsb - cbb * hb + ssb]
                return tuple(out)
            return step

        carry = tuple(jnp.zeros((batch, V7X_LANES), F32) for _ in range(4 * S5_PASS_GROUPS))
        carry = lax.fori_loop(0, n_ctx_chunks, make_step(0, n_ctx_chunks, ctx_pitch), carry)
        lax.fori_loop(0, n_lat_chunks, make_step(lat_base, n_lat_chunks, lat_pitch), carry)

        def lat_states(gi, cb):
            return jnp.concatenate([s_scr[gi, cb, lat_base + lat_pitch * b:lat_base + lat_pitch * b + n_lat_chunks, :]
                                    for b in range(batch)], axis=0)

        for gi in range(S5_PASS_GROUPS):
            g = g0 + gi
            hin = jnp.concatenate([lat_states(gi, 0), lat_states(gi, 2)], axis=1)
            y = jnp.dot(u_scr[g, ctx_rows:, :], m_scr[g], preferred_element_type=F32)
            y = y + lax.dot_general(hin.astype(BF16), rt_ref[g], _DN_LAST, preferred_element_type=F32)
            y = _gelu_tanh(y + dtab_ref[g] * u_scr[g, ctx_rows:, :].astype(F32))
            even = lax.bitcast_convert_type(y[:, :V7X_LANES].astype(BF16).astype(F32), jnp.uint32)
            odd = lax.bitcast_convert_type(y[:, V7X_LANES:].astype(BF16).astype(F32), jnp.uint32)
            y_scr[g] = (even >> 16) | odd

    def scatter(i, carry):
        r0 = pl.multiple_of(i * rc, rc)
        ys = [y_scr[g, pl.ds(r0, rc), :] for g in range(ng)]
        for k, x in enumerate(_transpose_lane_blocks(ys, lane)):
            w_scr[pl.ds((ctx_rows + r0) * pairs + k, rc, stride=pairs), :] = x
        return carry

    lax.fori_loop(0, lat_rows // rc, scatter, 0, unroll=2)
    o_ref[...] = pltpu.bitcast(w_scr[ctx_rows * pairs:, :], o_ref.dtype)


def _s5_mix(u_ctx, p_lat, st, rt, kc, bt, coef, dtab, pm, *, batch, n_ctx, seq):
    ng = S5_BLOCK_GROUPS
    width = u_ctx.shape[1]
    n_ctx_chunks, n_lat_chunks = n_ctx // CHUNK, seq // CHUNK
    rows = batch * (n_ctx_chunks + n_lat_chunks)
    lat_rows = batch * n_lat_chunks
    kern = functools.partial(_s5_kernel, batch=batch, n_ctx_chunks=n_ctx_chunks, n_lat_chunks=n_lat_chunks)
    return pl.pallas_call(
        kern,
        grid=(width // V7X_LANES,),
        in_specs=[
            pl.BlockSpec((batch * n_ctx, V7X_LANES), lambda i: (0, i)),
            pl.BlockSpec((batch * seq, V7X_LANES), lambda i: (0, i)),
            pl.BlockSpec((ng,) + dtab.shape[1:], lambda i: (i, 0, 0)),
            pl.BlockSpec((ng,) + st.shape[1:], lambda i: (i, 0, 0)),
            pl.BlockSpec((ng,) + rt.shape[1:], lambda i: (i, 0, 0)),
            pl.BlockSpec((ng,) + kc.shape[1:], lambda i: (i, 0, 0, 0)),
            pl.BlockSpec((ng,) + bt.shape[1:], lambda i: (i, 0, 0, 0)),
            pl.BlockSpec((ng,) + coef.shape[1:], lambda i: (i, 0, 0)),
            pl.BlockSpec(pm.shape, lambda i: (0, 0)),
        ],
        out_specs=pl.BlockSpec((batch * seq, V7X_LANES), lambda i: (0, i)),
        out_shape=jax.ShapeDtypeStruct((batch * seq, width), BF16),
        scratch_shapes=[
            pltpu.VMEM((rows * CHUNK // 2, V7X_LANES), jnp.uint32),
            pltpu.VMEM((ng, rows, CHUNK_W), BF16),
            pltpu.VMEM((S5_PASS_GROUPS, 4, _s5_state_rows(batch, n_ctx_chunks, n_lat_chunks)[3], V7X_LANES), F32),
            pltpu.VMEM((ng, lat_rows, V7X_LANES), jnp.uint32),
            pltpu.VMEM((ng, CHUNK_W, CHUNK_W), BF16),
            pltpu.VMEM((CHUNK_W, CHUNK_W), BF16),
        ],
        compiler_params=_params(("arbitrary",)),
        name="s5_mix",
    )(u_ctx, p_lat, dtab, st, rt, kc, bt, coef, pm)


def _s5_tables(A_re, A_im, log_dt, B_re, B_im, C_re, C_im, d_skip):
    g = A_re.shape[1]
    order = jnp.array(sorted(range(CHUNK), key=_token_slot))
    dt = jnp.exp(log_dt)[..., None]
    mag = jnp.exp(A_re * dt)
    a_re = mag * jnp.cos(A_im * dt)
    a_im = mag * jnp.sin(A_im * dt)
    den = A_re * A_re + A_im * A_im
    f_re = ((a_re - 1.0) * A_re + a_im * A_im) / den
    f_im = (a_im * A_re - (a_re - 1.0) * A_im) / den
    bb_re = f_re[..., None] * B_re - f_im[..., None] * B_im
    bb_im = f_re[..., None] * B_im + f_im[..., None] * B_re

    pr, pi = [jnp.ones_like(a_re)], [jnp.zeros_like(a_im)]
    for _ in range(CHUNK):
        pr, pi = pr + [pr[-1] * a_re - pi[-1] * a_im], pi + [pr[-1] * a_im + pi[-1] * a_re]
    pw_re, pw_im = jnp.stack(pr, axis=2), jnp.stack(pi, axis=2)

    ca_re = C_re[:, :, None] * pw_re[:, :, :, None, :] - C_im[:, :, None] * pw_im[:, :, :, None, :]
    ca_im = C_re[:, :, None] * pw_im[:, :, :, None, :] + C_im[:, :, None] * pw_re[:, :, :, None, :]
    ca = jnp.concatenate([ca_re, -ca_im], axis=-1)

    kc = jnp.stack([ca[0][:, :CHUNK], ca[1][:, CHUNK - 1::-1]], axis=1).reshape(g, 2, CHUNK_W, 2 * S5_STATE)
    rt = jnp.concatenate([ca[0][:, order + 1].reshape(g, CHUNK_W, 2 * S5_STATE),
                          ca[1][:, CHUNK - order].reshape(g, CHUNK_W, 2 * S5_STATE)], axis=-1).astype(BF16)
    bt = jnp.concatenate([bb_re.transpose(0, 1, 3, 2), bb_im.transpose(0, 1, 3, 2)], axis=-1)
    bt = bt.transpose(1, 0, 2, 3)

    def state_rows(d, powers):
        qr = pw_re[d][:, powers].transpose(0, 2, 1)[..., None]
        qi = pw_im[d][:, powers].transpose(0, 2, 1)[..., None]
        br, bi = bb_re[d][:, :, None, :], bb_im[d][:, :, None, :]
        s_re = (qr * br - qi * bi).reshape(g, S5_STATE, CHUNK_W)
        s_im = (qr * bi + qi * br).reshape(g, S5_STATE, CHUNK_W)
        return jnp.concatenate([s_re, s_im, s_im, s_re], axis=1)

    st = jnp.concatenate([state_rows(0, CHUNK - 1 - order), state_rows(1, order)], axis=1).astype(BF16)

    ar, ai = pw_re[:, :, CHUNK], pw_im[:, :, CHUNK]
    c_a = jnp.concatenate([ar, ar], axis=-1)
    c_b = jnp.concatenate([-ai, ai], axis=-1)
    coef = jnp.stack([c_a[0], c_b[0], c_a[1], c_b[1]], axis=1)

    dtab = jnp.tile(d_skip.reshape(g, 1, S5_GROUP), (1, 1, CHUNK))
    col = jnp.arange(CHUNK_W)
    dst = jnp.array([_token_slot(t) for t in range(CHUNK)])[col // S5_GROUP] * S5_GROUP + col % S5_GROUP
    pm = (dst[:, None] == col[None, :]).astype(BF16)
    return st, rt, kc, bt, coef, dtab, pm


def _rope_tables(n_tokens):
    rows = n_tokens // GRID_W
    row = jnp.repeat(jnp.arange(rows), GRID_W).astype(F32)
    col = jnp.tile(jnp.arange(GRID_W), rows).astype(F32)
    n_freq = HEAD_DIM // 4
    inv_freq = ROPE_BASE ** (-jnp.arange(n_freq, dtype=F32) / n_freq)
    ar = row[:, None] * inv_freq
    ac = col[:, None] * inv_freq
    cos_t = jnp.concatenate([jnp.cos(ar), jnp.cos(ar), jnp.cos(ac), jnp.cos(ac)], axis=1)
    sin_t = jnp.concatenate([-jnp.sin(ar), jnp.sin(ar), -jnp.sin(ac), jnp.sin(ac)], axis=1)
    return cos_t, sin_t


def kernel(x, c, ctx, c_ctx, ada_w, ada_b, norm_pre, norm_post, attn_w_in, attn_w_out, attn_lam, attn_subln, s5_w_in, s5_A_re, s5_A_im, s5_log_dt, s5_B_re, s5_B_im, s5_C_re, s5_C_im, s5_D, s5_w_glu, s5_w_out):
    bsz, seq, d = x.shape
    n_ctx = ctx.shape[1]
    depth = ada_w.shape[0]
    assert d == D_MODEL and depth == 2 and bsz + 1 <= MOD_ROWS
    ctx_row = bsz

    cc = jnp.concatenate([c, c_ctx[None, :], jnp.zeros((MOD_ROWS - bsz - 1, d), F32)], axis=0)
    mods = _modulation(cc, ada_w, ada_b)
    mods = mods.reshape(depth, MOD_ROWS, 3, 1, d).transpose(0, 2, 1, 3, 4)

    x_lat = x.reshape(bsz * seq, d)
    x_ctx = ctx.reshape(bsz * n_ctx, d)
    norm_bm = 256
    lat_tiles = seq // norm_bm

    shift, scale, gate = mods[0, 0], mods[0, 1], mods[0, 2]
    h_lat = _normmod(x_lat, norm_pre[0], scale, shift, tiles_per_batch=lat_tiles, bm=norm_bm)
    h_ctx = _normmod(x_ctx, norm_pre[0], scale, shift, fixed_row=ctx_row, bm=norm_bm)

    w_in = attn_w_in[0].astype(BF16)
    w_qk, w_vg = w_in[:, :2 * QK_W], w_in[:, 2 * QK_W:]
    cos_t, sin_t = _rope_tables(seq)
    qk_lat = _matmul_qk(h_lat, w_qk, cos_t, sin_t, rope=True)
    vg_lat = _matmul(h_lat, w_vg)
    dummy = jnp.zeros((V7X_SUBLANES, HEAD_DIM), F32)
    qk_ctx = _matmul_qk(h_ctx, w_qk, dummy, dummy, rope=False)
    vg_ctx = _matmul(h_ctx, w_vg)

    lam_init = 0.8 - 0.6 * math.exp(-0.3 * 0)
    a_lat = _attention(attn_lam[0], attn_subln[0], qk_lat, vg_lat,
                       [(qk_ctx, vg_ctx, n_ctx), (qk_lat, vg_lat, seq)],
                       batch=bsz, n_q=seq, tq=512, lam_init=lam_init)
    a_ctx = _attention(attn_lam[0], attn_subln[0], qk_ctx, vg_ctx,
                       [(qk_ctx, vg_ctx, n_ctx)],
                       batch=bsz, n_q=n_ctx, tq=n_ctx, lam_init=lam_init)
    w_out = attn_w_out[0].astype(BF16)
    o_lat = _matmul(a_lat, w_out)
    o_ctx = _matmul(a_ctx, w_out)
    x_lat = _resid(x_lat, o_lat, norm_post[0], gate, tiles_per_batch=lat_tiles, bm=norm_bm)
    x_ctx = _resid(x_ctx, o_ctx, norm_post[0], gate, fixed_row=ctx_row, bm=norm_bm)

    shift, scale, gate = mods[1, 0], mods[1, 1], mods[1, 2]
    h_lat = _normmod(x_lat, norm_pre[1], scale, shift, tiles_per_batch=lat_tiles, bm=norm_bm)
    h_ctx = _normmod(x_ctx, norm_pre[1], scale, shift, fixed_row=ctx_row, bm=norm_bm)
    w_in = s5_w_in[0].astype(BF16)
    p_lat = _matmul(h_lat, w_in)
    u_ctx = _matmul(h_ctx, w_in[:, :d])

    tables = _s5_tables(s5_A_re[0], s5_A_im[0], s5_log_dt[0], s5_B_re[0], s5_B_im[0],
                        s5_C_re[0], s5_C_im[0], s5_D[0])
    yact = _s5_mix(u_ctx, p_lat, *tables, batch=bsz, n_ctx=n_ctx, seq=seq)
    gl = _matmul_glu(yact, s5_w_glu[0].astype(BF16), p_lat)
    o_lat = _matmul(gl, s5_w_out[0].astype(BF16))
    x_lat = _resid(x_lat, o_lat, norm_post[1], gate, tiles_per_batch=lat_tiles, bm=norm_bm)
    return x_lat.reshape(bsz, seq, d)
```

```python
import functools
import math

import jax
import jax.numpy as jnp
from jax import lax
from jax.experimental import pallas as pl
from jax.experimental.pallas import tpu as pltpu

F32 = jnp.float32
BF16 = jnp.bfloat16

V7X_LANES = 128
V7X_SUBLANES = 8
V7X_VMEM_BYTES = 64 * 1024 * 1024
VMEM_LIMIT = V7X_VMEM_BYTES - 8 * 1024 * 1024

D_MODEL = 4096
GRID_W = 64
EPS = 1e-6
HEADS = 16
HEAD_DIM = 128
V_DIM = 2 * HEAD_DIM
QK_W = HEADS * 2 * HEAD_DIM
ROPE_BASE = 10000.0
S5_GROUP = 16
S5_GROUPS = D_MODEL // S5_GROUP
S5_STATE = 64
CHUNK = 16
CHUNK_W = CHUNK * S5_GROUP
MOD_ROWS = 8
_DN_LAST = (((1,), (1,)), ((), ()))


def _params(sem, vmem=VMEM_LIMIT):
    return pltpu.CompilerParams(dimension_semantics=sem, vmem_limit_bytes=vmem)


def _sigmoid(x):
    return 1.0 / (1.0 + jnp.exp(-x))


def _silu(x):
    return x * _sigmoid(x)


def _mod_kernel(c_ref, w_ref, b_ref, o_ref):
    a = _silu(c_ref[...])
    acc = jnp.dot(a, w_ref[0], preferred_element_type=F32, precision=lax.Precision.HIGHEST)
    o_ref[0] = acc + b_ref[0]


def _modulation(cc, ada_w, ada_b):
    depth, d, n = ada_w.shape
    bn = 1024
    return pl.pallas_call(
        _mod_kernel,
        grid=(depth, n // bn),
        in_specs=[
            pl.BlockSpec((MOD_ROWS, d), lambda i, j: (0, 0)),
            pl.BlockSpec((1, d, bn), lambda i, j: (i, 0, j)),
            pl.BlockSpec((1, 1, bn), lambda i, j: (i, 0, j)),
        ],
        out_specs=pl.BlockSpec((1, MOD_ROWS, bn), lambda i, j: (i, 0, j)),
        out_shape=jax.ShapeDtypeStruct((depth, MOD_ROWS, n), F32),
        compiler_params=_params(("arbitrary", "arbitrary")),
        name="modulation",
    )(cc, ada_w, ada_b.reshape(depth, 1, n))


def _rms(x):
    return x * lax.rsqrt(jnp.mean(x * x, axis=-1, keepdims=True) + EPS)


def _normmod_kernel(x_ref, g_ref, scale_ref, shift_ref, o_ref):
    y = _rms(x_ref[...]) * g_ref[...]
    o_ref[...] = (y * (1.0 + scale_ref[0]) + shift_ref[0]).astype(o_ref.dtype)


def _resid_kernel(x_ref, o_in_ref, g_ref, gate_ref, o_ref):
    y = _rms(o_in_ref[...].astype(F32)) * g_ref[...]
    o_ref[...] = x_ref[...] + gate_ref[0] * y


def _resid_normmod_kernel(x_ref, o_in_ref, g_post_ref, gate_ref, g_pre_ref, scale_ref, shift_ref, x_out_ref, h_ref):
    y = _rms(o_in_ref[...].astype(F32)) * g_post_ref[...]
    x = x_ref[...] + gate_ref[0] * y
    x_out_ref[...] = x
    h = _rms(x) * g_pre_ref[...]
    h_ref[...] = (h * (1.0 + scale_ref[0]) + shift_ref[0]).astype(h_ref.dtype)


def _mod_row_map(rows_per_mod_row, fixed_row):
    if fixed_row is not None:
        return lambda i: (fixed_row, 0, 0)
    return lambda i: (i // rows_per_mod_row, 0, 0)


def _normmod(x2, g, scale3, shift3, *, tiles_per_batch=None, fixed_row=None, bm=256):
    m, d = x2.shape
    row_map = _mod_row_map(tiles_per_batch, fixed_row)
    return pl.pallas_call(
        _normmod_kernel,
        grid=(m // bm,),
        in_specs=[
            pl.BlockSpec((bm, d), lambda i: (i, 0)),
            pl.BlockSpec((1, d), lambda i: (0, 0)),
            pl.BlockSpec((1, 1, d), row_map),
            pl.BlockSpec((1, 1, d), row_map),
        ],
        out_specs=pl.BlockSpec((bm, d), lambda i: (i, 0)),
        out_shape=jax.ShapeDtypeStruct((m, d), BF16),
        compiler_params=_params(("arbitrary",)),
        name="normmod",
    )(x2, g.reshape(1, d), scale3, shift3)


def _resid_normmod(x2, o2, g_post, gate3, g_pre, scale3, shift3, *, tiles_per_batch=None, fixed_row=None, bm=256):
    m, d = x2.shape
    row_map = _mod_row_map(tiles_per_batch, fixed_row)
    tile = pl.BlockSpec((bm, d), lambda i: (i, 0))
    vec = pl.BlockSpec((1, d), lambda i: (0, 0))
    mod = pl.BlockSpec((1, 1, d), row_map)
    return pl.pallas_call(
        _resid_normmod_kernel,
        grid=(m // bm,),
        in_specs=[tile, tile, vec, mod, vec, mod, mod],
        out_specs=[tile, tile],
        out_shape=[jax.ShapeDtypeStruct((m, d), F32), jax.ShapeDtypeStruct((m, d), BF16)],
        compiler_params=_params(("arbitrary",)),
        name="resid_normmod",
    )(x2, o2, g_post.reshape(1, d), gate3, g_pre.reshape(1, d), scale3, shift3)


def _resid(x2, o2, g, gate3, *, tiles_per_batch=None, fixed_row=None, bm=256):
    m, d = x2.shape
    row_map = _mod_row_map(tiles_per_batch, fixed_row)
    return pl.pallas_call(
        _resid_kernel,
        grid=(m // bm,),
        in_specs=[
            pl.BlockSpec((bm, d), lambda i: (i, 0)),
            pl.BlockSpec((bm, d), lambda i: (i, 0)),
            pl.BlockSpec((1, d), lambda i: (0, 0)),
            pl.BlockSpec((1, 1, d), row_map),
        ],
        out_specs=pl.BlockSpec((bm, d), lambda i: (i, 0)),
        out_shape=jax.ShapeDtypeStruct((m, d), F32),
        compiler_params=_params(("arbitrary",)),
        name="resid",
    )(x2, o2, g.reshape(1, d), gate3)


def _mm_kernel(x_ref, w_ref, o_ref):
    acc = jnp.dot(x_ref[...], w_ref[...], preferred_element_type=F32)
    o_ref[...] = acc.astype(o_ref.dtype)


def _mm_qk_kernel(x_ref, w_ref, cos_ref, sin_ref, o_ref, *, rope, q_tiles, q_scale):
    acc = jnp.dot(x_ref[...], w_ref[...], preferred_element_type=F32)
    sc = jnp.where(pl.program_id(1) < q_tiles, q_scale, 1.0).astype(F32)
    bn = acc.shape[1]
    if rope:
        cos = cos_ref[...]
        sin = sin_ref[...]
        lane = lax.broadcasted_iota(jnp.int32, (1, HEAD_DIM), 1)
        first = (lane % (HEAD_DIM // 2)) < (HEAD_DIM // 4)
    for c in range(bn // HEAD_DIM):
        xc = acc[:, c * HEAD_DIM:(c + 1) * HEAD_DIM]
        if rope:
            partner = jnp.where(first,
                                pltpu.roll(xc, HEAD_DIM - HEAD_DIM // 4, 1),
                                pltpu.roll(xc, HEAD_DIM // 4, 1))
            xc = xc * cos + partner * sin
        o_ref[:, c * HEAD_DIM:(c + 1) * HEAD_DIM] = (xc * sc).astype(o_ref.dtype)


def _mm_glu_kernel(x_ref, w_ref, y_ref, z_ref, o_ref):
    acc = jnp.dot(x_ref[...], w_ref[...], preferred_element_type=F32)
    y = y_ref[...].astype(F32)
    o_ref[...] = (y * _sigmoid(acc) * _silu(z_ref[...].astype(F32))).astype(o_ref.dtype)


def _mm_tiles(m, n):
    return min(m, 1024), min(n, 1024)


def _matmul(x, w, out_dtype=BF16):
    m, k = x.shape
    n = w.shape[1]
    bm, bn = _mm_tiles(m, n)
    return pl.pallas_call(
        _mm_kernel,
        grid=(m // bm, n // bn),
        in_specs=[pl.BlockSpec((bm, k), lambda i, j: (i, 0)),
                  pl.BlockSpec((k, bn), lambda i, j: (0, j))],
        out_specs=pl.BlockSpec((bm, bn), lambda i, j: (i, j)),
        out_shape=jax.ShapeDtypeStruct((m, n), out_dtype),
        compiler_params=_params(("arbitrary", "arbitrary")),
        name="proj",
    )(x, w)


def _matmul_qk(x, w, cos_t, sin_t, *, rope):
    m, k = x.shape
    n = w.shape[1]
    bm, bn = _mm_tiles(m, n)
    tiles_per_seq = cos_t.shape[0] // bm if rope else 1
    tab_map = (lambda i, j: (i % tiles_per_seq, 0)) if rope else (lambda i, j: (0, 0))
    tab_rows = bm if rope else V7X_SUBLANES
    kern = functools.partial(_mm_qk_kernel, rope=rope, q_tiles=(n // 2) // bn,
                             q_scale=HEAD_DIM ** -0.5 * math.log2(math.e))
    return pl.pallas_call(
        kern,
        grid=(m // bm, n // bn),
        in_specs=[pl.BlockSpec((bm, k), lambda i, j: (i, 0)),
                  pl.BlockSpec((k, bn), lambda i, j: (0, j)),
                  pl.BlockSpec((tab_rows, HEAD_DIM), tab_map),
                  pl.BlockSpec((tab_rows, HEAD_DIM), tab_map)],
        out_specs=pl.BlockSpec((bm, bn), lambda i, j: (i, j)),
        out_shape=jax.ShapeDtypeStruct((m, n), BF16),
        compiler_params=_params(("arbitrary", "arbitrary")),
        name="proj_qk_rope" if rope else "proj_qk",
    )(x, w, cos_t, sin_t)


def _matmul_glu(y, w, p):
    m, k = y.shape
    n = w.shape[1]
    bm, bn = _mm_tiles(m, n)
    z_off = n // bn
    return pl.pallas_call(
        _mm_glu_kernel,
        grid=(m // bm, n // bn),
        in_specs=[pl.BlockSpec((bm, k), lambda i, j: (i, 0)),
                  pl.BlockSpec((k, bn), lambda i, j: (0, j)),
                  pl.BlockSpec((bm, bn), lambda i, j: (i, j)),
                  pl.BlockSpec((bm, bn), lambda i, j: (i, z_off + j))],
        out_specs=pl.BlockSpec((bm, bn), lambda i, j: (i, j)),
        out_shape=jax.ShapeDtypeStruct((m, n), BF16),
        compiler_params=_params(("arbitrary", "arbitrary")),
        name="proj_glu",
    )(y, w, y, p)


ATT_KEY_CHUNK = 256
ATT_PV_GROUP = 1
ATT_LAG = 4


def _lane_block_reduce(op, x):
    return functools.reduce(op, [x[:, V7X_LANES * j:V7X_LANES * (j + 1)] for j in range(x.shape[1] // V7X_LANES)])


def _attn_kernel(lam_ref, g_ref, q_ref, gate_ref, *refs, n_seg, lam_init):
    o_ref, s_scr = refs[2 * n_seg:]
    segs = [(refs[2 * i], refs[2 * i + 1]) for i in range(n_seg)]
    lv = lam_ref[...]
    t1 = jnp.sum(lv[0:1] * lv[1:2], axis=1, keepdims=True)
    t2 = jnp.sum(lv[2:3] * lv[3:4], axis=1, keepdims=True)
    lam = jnp.exp(t1) - jnp.exp(t2) + lam_init

    ch = s_scr.shape[3]
    chunks = []
    for si, (k, _) in enumerate(segs):
        for start in range(0, k.shape[0], ch):
            chunks.append((si, start, min(ch, k.shape[0] - start), len(chunks)))
    n_main = segs[-1][0].shape[0] // ch
    n_head = len(chunks) - n_main

    q = [q_ref[:, :HEAD_DIM], q_ref[:, HEAD_DIM:]]
    mrun = [None, None]
    for si, start, size, c in chunks:
        k_ref = segs[si][0]
        for n in range(2):
            s = lax.dot_general(q[n], k_ref[start:start + size, HEAD_DIM * n:HEAD_DIM * (n + 1)],
                                _DN_LAST, preferred_element_type=F32)
            s_scr[n, c, :, :size] = s
            part = _lane_block_reduce(jnp.maximum, s)
            mrun[n] = part if mrun[n] is None else jnp.maximum(mrun[n], part)
    m = [jnp.max(mrun[n], axis=1, keepdims=True) for n in range(2)]

    tq = q_ref.shape[0]
    m_bits = [pltpu.bitcast(jnp.broadcast_to(m[n], (tq, V7X_LANES)), jnp.uint32) for n in range(2)]
    lrun = [None, None]
    acc = [None, None]
    hist = [[], []]
    groups = []
    for si in range(n_seg):
        own = [ck for ck in chunks if ck[0] == si]
        groups += [own[i:i + ATT_PV_GROUP] for i in range(0, len(own), ATT_PV_GROUP)]
    for idx, grp in enumerate(groups):
        si, start = grp[0][0], grp[0][1]
        v = segs[si][1][start:start + sum(ck[2] for ck in grp), :]
        for n in range(2):
            mb = m_bits[n]
            if idx >= ATT_LAG:
                mb = (mb.reshape(tq // V7X_SUBLANES, V7X_SUBLANES, V7X_LANES) | hist[n][idx - ATT_LAG][None])
                mb = mb.reshape(tq, V7X_LANES)
            mc = pltpu.bitcast(mb, F32)
            ps = [jnp.exp2(s_scr[n, c, :, V7X_LANES * j:V7X_LANES * (j + 1)] - mc)
                  for _, _, size, c in grp for j in range(size // V7X_LANES)]
            part = functools.reduce(jnp.add, ps)
            lrun[n] = part if lrun[n] is None else lrun[n] + part
            pv = jnp.dot(jnp.concatenate(ps, axis=1).astype(BF16), v, preferred_element_type=F32)
            acc[n] = pv if acc[n] is None else acc[n] + pv
            tail = pltpu.bitcast(pv[tq - V7X_SUBLANES:, V_DIM - V7X_LANES:], jnp.uint32)
            hist[n].append((tail >> 16) >> 16)
    l1 = jnp.sum(lrun[0], axis=1, keepdims=True)
    l2 = jnp.sum(lrun[1], axis=1, keepdims=True)
    o = acc[0] * (1.0 / l1) - acc[1] * (lam / l2)
    o = _rms(o) * g_ref[...] * (1.0 - lam_init)
    o_ref[...] = (o * _silu(gate_ref[...].astype(F32))).astype(o_ref.dtype)


def _attention(lam_vecs, subln_g, q_arr, gate_arr, segs, *, batch, n_q, tq, lam_init):
    nq_tiles = n_q // tq
    n_seg = len(segs)
    in_specs = [
        pl.BlockSpec((4, HEAD_DIM), lambda b, h, i: (0, 0)),
        pl.BlockSpec((1, V_DIM), lambda b, h, i: (0, 0)),
        pl.BlockSpec((tq, V_DIM), lambda b, h, i: (b * nq_tiles + i, h)),
        pl.BlockSpec((tq, V_DIM), lambda b, h, i: (b * nq_tiles + i, HEADS + h)),
    ]
    args = [lam_vecs, subln_g.reshape(1, V_DIM), q_arr, gate_arr]
    for k_arr, v_arr, rows in segs:
        in_specs.append(pl.BlockSpec((rows, V_DIM), lambda b, h, i: (b, HEADS + h)))
        in_specs.append(pl.BlockSpec((rows, V_DIM), lambda b, h, i: (b, h)))
        args += [k_arr, v_arr]
    kern = functools.partial(_attn_kernel, n_seg=n_seg, lam_init=lam_init)
    ch = min(ATT_KEY_CHUNK, max(rows for _, _, rows in segs))
    n_chunks = sum(pl.cdiv(rows, ch) for _, _, rows in segs)
    return pl.pallas_call(
        kern,
        grid=(batch, HEADS, nq_tiles),
        in_specs=in_specs,
        out_specs=pl.BlockSpec((tq, V_DIM), lambda b, h, i: (b * nq_tiles + i, h)),
        out_shape=jax.ShapeDtypeStruct((batch * n_q, HEADS * V_DIM), BF16),
        scratch_shapes=[pltpu.VMEM((2, n_chunks, tq, ch), F32)],
        compiler_params=_params(("arbitrary", "arbitrary", "arbitrary")),
        name="diff_attn",
    )(*args)


S5_BLOCK_GROUPS = V7X_LANES // S5_GROUP
S5_PASS_GROUPS = 4
S5_ROW_CHUNK = 32


def _shift_lanes_256(x, n, lane):
    a, b = x[:, :V7X_LANES], x[:, V7X_LANES:]
    zero = jnp.zeros_like(a)
    if n == 0:
        return x
    if n > 0:
        q, r = divmod(n, V7X_LANES)
        if r == 0:
            lo, hi = zero, a
        else:
            ra, rb = pltpu.roll(a, r, 1), pltpu.roll(b, r, 1)
            keep = lane >= r
            if q == 0:
                lo, hi = jnp.where(keep, ra, 0.0), jnp.where(keep, rb, ra)
            else:
                lo, hi = zero, jnp.where(keep, ra, 0.0)
    else:
        q, r = divmod(-n, V7X_LANES)
        if r == 0:
            lo, hi = b, zero
        else:
            ra, rb = pltpu.roll(a, V7X_LANES - r, 1), pltpu.roll(b, V7X_LANES - r, 1)
            keep = lane < V7X_LANES - r
            if q == 0:
                lo, hi = jnp.where(keep, ra, rb), jnp.where(keep, rb, 0.0)
            else:
                lo, hi = jnp.where(keep, rb, 0.0), zero
    return jnp.concatenate([lo, hi], axis=1)


def _token_slot(t):
    return (t % 2) * (CHUNK // 2) + t // 2


def _transpose_lane_blocks(vs, lane):
    vs = list(vs)
    d = len(vs) // 2
    while d >= 1:
        low = (lane // (S5_GROUP * d)) % 2 == 0
        for i in range(len(vs)):
            if i & d == 0:
                x, y = vs[i], vs[i + d]
                vs[i] = jnp.where(low, x, pltpu.roll(y, S5_GROUP * d, 1))
                vs[i + d] = jnp.where(low, pltpu.roll(x, V7X_LANES - S5_GROUP * d, 1), y)
        d //= 2
    return vs


def _s5_state_rows(batch, n_ctx_chunks, n_lat_chunks):
    ctx_pitch = n_ctx_chunks + V7X_SUBLANES
    lat_pitch = n_lat_chunks + V7X_SUBLANES
    lat_base = batch * ctx_pitch
    return ctx_pitch, lat_pitch, lat_base, lat_base + batch * lat_pitch


def _gelu_tanh(y):
    inner = math.sqrt(2.0 / math.pi) * (y + 0.044715 * (y * y * y))
    return 0.5 * y * (1.0 + jnp.tanh(inner))


def _s5_kernel(uctx_ref, ulat_ref, dtab_ref, st_ref, rt_ref, kc_ref, bt_ref, coef_ref, pm_ref, o_ref,
               w_scr, u_scr, s_scr, y_scr, m_scr, mt_scr, *, batch, n_ctx_chunks, n_lat_chunks):
    ng = S5_BLOCK_GROUPS
    rc = S5_ROW_CHUNK
    pairs = CHUNK // 2
    ctx_rows = batch * n_ctx_chunks
    lat_rows = batch * n_lat_chunks
    ctx_pitch, lat_pitch, lat_base = _s5_state_rows(batch, n_ctx_chunks, n_lat_chunks)[:3]
    lane = lax.broadcasted_iota(jnp.int32, (1, V7X_LANES), 1)

    w_scr[0:ctx_rows * pairs, :] = pltpu.bitcast(uctx_ref[...], jnp.uint32)
    w_scr[ctx_rows * pairs:, :] = pltpu.bitcast(ulat_ref[...], jnp.uint32)

    def gather(i, carry):
        r0 = pl.multiple_of(i * rc, rc)
        vs = [w_scr[pl.ds(r0 * pairs + k, rc, stride=pairs), :] for k in range(pairs)]
        for g, x in enumerate(_transpose_lane_blocks(vs, lane)):
            for par, bits in enumerate((x << 16, x & jnp.uint32(0xFFFF0000))):
                tok = lax.bitcast_convert_type(bits, F32)
                u_scr[g, pl.ds(r0, rc), V7X_LANES * par:V7X_LANES * (par + 1)] = tok.astype(BF16)
        return carry

    lax.fori_loop(0, (ctx_rows + lat_rows) // rc, gather, 0, unroll=2)

    for g in range(ng):
        kf = lax.dot_general(bt_ref[g, 0], kc_ref[g, 0], _DN_LAST, preferred_element_type=F32,
                             precision=lax.Precision.HIGHEST)
        kb = lax.dot_general(bt_ref[g, 1], kc_ref[g, 1], _DN_LAST, preferred_element_type=F32,
                             precision=lax.Precision.HIGHEST)
        for s in range(CHUNK):
            blk = (_shift_lanes_256(kf, S5_GROUP * s, lane)
                   + _shift_lanes_256(kb, -S5_GROUP * (CHUNK - 1 - s), lane))
            pos = _token_slot(s)
            mt_scr[S5_GROUP * pos:S5_GROUP * (pos + 1), :] = blk.astype(BF16)
        m_scr[g] = jnp.dot(mt_scr[...], pm_ref[...], preferred_element_type=F32).astype(BF16)

    for p in range(ng // S5_PASS_GROUPS):
        g0 = p * S5_PASS_GROUPS
        for gi in range(S5_PASS_GROUPS):
            o1 = lax.dot_general(u_scr[g0 + gi], st_ref[g0 + gi], _DN_LAST, preferred_element_type=F32)
            for cb in range(4):
                col = o1[:, V7X_LANES * cb:V7X_LANES * (cb + 1)]
                for b in range(batch):
                    s_scr[gi, cb, ctx_pitch * b:ctx_pitch * b + n_ctx_chunks, :] = (
                        col[n_ctx_chunks * b:n_ctx_chunks * (b + 1)])
                    s_scr[gi, cb, lat_base + lat_pitch * b:lat_base + lat_pitch * b + n_lat_chunks, :] = (
                        col[ctx_rows + n_lat_chunks * b:ctx_rows + n_lat_chunks * (b + 1)])

        coefs = [[coef_ref[g0 + gi, i:i + 1, :] for i in range(4)] for gi in range(S5_PASS_GROUPS)]

        def make_step(base, n, pitch):
            def step(i, carry):
                out = []
                for gi in range(S5_PASS_GROUPS):
                    hf, hsf, hb, hsb = carry[4 * gi:4 * gi + 4]
                    rf = pl.ds(base + i, batch, stride=pitch)
                    rb = pl.ds(base + n - 1 - i, batch, stride=pitch)
                    sf, ssf = s_scr.at[gi, 0][rf, :], s_scr.at[gi, 1][rf, :]
                    sb, ssb = s_scr.at[gi, 2][rb, :], s_scr.at[gi, 3][rb, :]
                    s_scr.at[gi, 0][rf, :] = hf
                    s_scr.at[gi, 2][rb, :] = hb
                    caf, cbf, cab, cbb = coefs[gi]
                    out += [caf * hf + cbf * hsf + sf, caf * hsf - cbf * hf + ssf,
                            cab * hb + cbb * hsb + sb, cab * hsb - cbb * hb + ssb]
                return tuple(out)
            return step

        carry = tuple(jnp.zeros((batch, V7X_LANES), F32) for _ in range(4 * S5_PASS_GROUPS))
        carry = lax.fori_loop(0, n_ctx_chunks, make_step(0, n_ctx_chunks, ctx_pitch), carry)
        lax.fori_loop(0, n_lat_chunks, make_step(lat_base, n_lat_chunks, lat_pitch), carry)

        def lat_states(gi, cb):
            return jnp.concatenate([s_scr[gi, cb, lat_base + lat_pitch * b:lat_base + lat_pitch * b + n_lat_chunks, :]
                                    for b in range(batch)], axis=0)

        for gi in range(S5_PASS_GROUPS):
            g = g0 + gi
            hin = jnp.concatenate([lat_states(gi, 0), lat_states(gi, 2)], axis=1)
            y = jnp.dot(u_scr[g, ctx_rows:, :], m_scr[g], preferred_element_type=F32)
            y = y + lax.dot_general(hin.astype(BF16), rt_ref[g], _DN_LAST, preferred_element_type=F32)
            y = _gelu_tanh(y + dtab_ref[g] * u_scr[g, ctx_rows:, :].astype(F32))
            even = lax.bitcast_convert_type(y[:, :V7X_LANES].astype(BF16).astype(F32), jnp.uint32)
            odd = lax.bitcast_convert_type(y[:, V7X_LANES:].astype(BF16).astype(F32), jnp.uint32)
            y_scr[g] = (even >> 16) | odd

    def scatter(i, carry):
        r0 = pl.multiple_of(i * rc, rc)
        ys = [y_scr[g, pl.ds(r0, rc), :] for g in range(ng)]
        for k, x in enumerate(_transpose_lane_blocks(ys, lane)):
            w_scr[pl.ds((ctx_rows + r0) * pairs + k, rc, stride=pairs), :] = x
        return carry

    lax.fori_loop(0, lat_rows // rc, scatter, 0, unroll=2)
    o_ref[...] = pltpu.bitcast(w_scr[ctx_rows * pairs:, :], o_ref.dtype)


def _s5_mix(u_ctx, p_lat, st, rt, kc, bt, coef, dtab, pm, *, batch, n_ctx, seq):
    ng = S5_BLOCK_GROUPS
    width = u_ctx.shape[1]
    n_ctx_chunks, n_lat_chunks = n_ctx // CHUNK, seq // CHUNK
    rows = batch * (n_ctx_chunks + n_lat_chunks)
    lat_rows = batch * n_lat_chunks
    kern = functools.partial(_s5_kernel, batch=batch, n_ctx_chunks=n_ctx_chunks, n_lat_chunks=n_lat_chunks)
    return pl.pallas_call(
        kern,
        grid=(width // V7X_LANES,),
        in_specs=[
            pl.BlockSpec((batch * n_ctx, V7X_LANES), lambda i: (0, i)),
            pl.BlockSpec((batch * seq, V7X_LANES), lambda i: (0, i)),
            pl.BlockSpec((ng,) + dtab.shape[1:], lambda i: (i, 0, 0)),
            pl.BlockSpec((ng,) + st.shape[1:], lambda i: (i, 0, 0)),
            pl.BlockSpec((ng,) + rt.shape[1:], lambda i: (i, 0, 0)),
            pl.BlockSpec((ng,) + kc.shape[1:], lambda i: (i, 0, 0, 0)),
            pl.BlockSpec((ng,) + bt.shape[1:], lambda i: (i, 0, 0, 0)),
            pl.BlockSpec((ng,) + coef.shape[1:], lambda i: (i, 0, 0)),
            pl.BlockSpec(pm.shape, lambda i: (0, 0)),
        ],
        out_specs=pl.BlockSpec((batch * seq, V7X_LANES), lambda i: (0, i)),
        out_shape=jax.ShapeDtypeStruct((batch * seq, width), BF16),
        scratch_shapes=[
            pltpu.VMEM((rows * CHUNK // 2, V7X_LANES), jnp.uint32),
            pltpu.VMEM((ng, rows, CHUNK_W), BF16),
            pltpu.VMEM((S5_PASS_GROUPS, 4, _s5_state_rows(batch, n_ctx_chunks, n_lat_chunks)[3], V7X_LANES), F32),
            pltpu.VMEM((ng, lat_rows, V7X_LANES), jnp.uint32),
            pltpu.VMEM((ng, CHUNK_W, CHUNK_W), BF16),
            pltpu.VMEM((CHUNK_W, CHUNK_W), BF16),
        ],
        compiler_params=_params(("arbitrary",)),
        name="s5_mix",
    )(u_ctx, p_lat, dtab, st, rt, kc, bt, coef, pm)


def _s5_tables(A_re, A_im, log_dt, B_re, B_im, C_re, C_im, d_skip):
    g = A_re.shape[1]
    order = sorted(range(CHUNK), key=_token_slot)
    dt = jnp.exp(log_dt)[..., None]
    mag = jnp.exp(A_re * dt)
    a_re = mag * jnp.cos(A_im * dt)
    a_im = mag * jnp.sin(A_im * dt)
    den = A_re * A_re + A_im * A_im
    f_re = ((a_re - 1.0) * A_re + a_im * A_im) / den
    f_im = (a_im * A_re - (a_re - 1.0) * A_im) / den
    bb_re = f_re[..., None] * B_re - f_im[..., None] * B_im
    bb_im = f_re[..., None] * B_im + f_im[..., None] * B_re

    pr, pi = [jnp.ones_like(a_re)], [jnp.zeros_like(a_im)]
    for _ in range(CHUNK):
        pr, pi = pr + [pr[-1] * a_re - pi[-1] * a_im], pi + [pr[-1] * a_im + pi[-1] * a_re]

    def ca(d, k):
        qr, qi = pr[k][d][:, None, :], pi[k][d][:, None, :]
        return jnp.concatenate([C_re[d] * qr - C_im[d] * qi, -(C_re[d] * qi + C_im[d] * qr)], axis=-1)

    def rows(blocks):
        return jnp.stack(blocks, axis=1).reshape(g, CHUNK_W, 2 * S5_STATE)

    kc = jnp.stack([rows([ca(0, k) for k in range(CHUNK)]),
                    rows([ca(1, CHUNK - 1 - k) for k in range(CHUNK)])], axis=1)
    rt = jnp.concatenate([rows([ca(0, t + 1) for t in order]),
                          rows([ca(1, CHUNK - t) for t in order])], axis=-1).astype(BF16)
    bt = jnp.concatenate([bb_re.transpose(0, 1, 3, 2), bb_im.transpose(0, 1, 3, 2)], axis=-1)
    bt = bt.transpose(1, 0, 2, 3)

    def state_rows(d, powers):
        qr = jnp.stack([pr[k][d] for k in powers], axis=-1)[..., None]
        qi = jnp.stack([pi[k][d] for k in powers], axis=-1)[..., None]
        br, bi = bb_re[d][:, :, None, :], bb_im[d][:, :, None, :]
        s_re = (qr * br - qi * bi).reshape(g, S5_STATE, CHUNK_W)
        s_im = (qr * bi + qi * br).reshape(g, S5_STATE, CHUNK_W)
        return jnp.concatenate([s_re, s_im, s_im, s_re], axis=1)

    st = jnp.concatenate([state_rows(0, [CHUNK - 1 - s for s in order]), state_rows(1, order)],
                         axis=1).astype(BF16)

    ar, ai = pr[CHUNK], pi[CHUNK]
    c_a = jnp.concatenate([ar, ar], axis=-1)
    c_b = jnp.concatenate([-ai, ai], axis=-1)
    coef = jnp.stack([c_a[0], c_b[0], c_a[1], c_b[1]], axis=1)

    dtab = jnp.tile(d_skip.reshape(g, 1, S5_GROUP), (1, 1, CHUNK))
    col = jnp.arange(CHUNK_W)
    dst = jnp.array([_token_slot(t) for t in range(CHUNK)])[col // S5_GROUP] * S5_GROUP + col % S5_GROUP
    pm = (dst[:, None] == col[None, :]).astype(BF16)
    return st, rt, kc, bt, coef, dtab, pm


def _rope_tables(n_tokens):
    rows = n_tokens // GRID_W
    row = jnp.repeat(jnp.arange(rows), GRID_W).astype(F32)
    col = jnp.tile(jnp.arange(GRID_W), rows).astype(F32)
    n_freq = HEAD_DIM // 4
    inv_freq = ROPE_BASE ** (-jnp.arange(n_freq, dtype=F32) / n_freq)
    ar = row[:, None] * inv_freq
    ac = col[:, None] * inv_freq
    cos_t = jnp.concatenate([jnp.cos(ar), jnp.cos(ar), jnp.cos(ac), jnp.cos(ac)], axis=1)
    sin_t = jnp.concatenate([-jnp.sin(ar), jnp.sin(ar), -jnp.sin(ac), jnp.sin(ac)], axis=1)
    return cos_t, sin_t


def kernel(x, c, ctx, c_ctx, ada_w, ada_b, norm_pre, norm_post, attn_w_in, attn_w_out, attn_lam, attn_subln, s5_w_in, s5_A_re, s5_A_im, s5_log_dt, s5_B_re, s5_B_im, s5_C_re, s5_C_im, s5_D, s5_w_glu, s5_w_out):
    bsz, seq, d = x.shape
    n_ctx = ctx.shape[1]
    depth = ada_w.shape[0]
    assert d == D_MODEL and depth == 2 and bsz + 1 <= MOD_ROWS
    ctx_row = bsz

    cc = jnp.concatenate([c, c_ctx[None, :], jnp.zeros((MOD_ROWS - bsz - 1, d), F32)], axis=0)
    mods = _modulation(cc, ada_w, ada_b)
    mods = mods.reshape(depth, MOD_ROWS, 3, 1, d).transpose(0, 2, 1, 3, 4)

    x_lat = x.reshape(bsz * seq, d)
    x_ctx = ctx.reshape(bsz * n_ctx, d)
    norm_bm = 256
    lat_tiles = seq // norm_bm

    shift, scale, gate = mods[0, 0], mods[0, 1], mods[0, 2]
    h_lat = _normmod(x_lat, norm_pre[0], scale, shift, tiles_per_batch=lat_tiles, bm=norm_bm)
    h_ctx = _normmod(x_ctx, norm_pre[0], scale, shift, fixed_row=ctx_row, bm=norm_bm)

    w_in = attn_w_in[0].astype(BF16)
    w_qk, w_vg = w_in[:, :2 * QK_W], w_in[:, 2 * QK_W:]
    cos_t, sin_t = _rope_tables(seq)
    qk_lat = _matmul_qk(h_lat, w_qk, cos_t, sin_t, rope=True)
    vg_lat = _matmul(h_lat, w_vg)
    dummy = jnp.zeros((V7X_SUBLANES, HEAD_DIM), F32)
    qk_ctx = _matmul_qk(h_ctx, w_qk, dummy, dummy, rope=False)
    vg_ctx = _matmul(h_ctx, w_vg)

    lam_init = 0.8 - 0.6 * math.exp(-0.3 * 0)
    a_lat = _attention(attn_lam[0], attn_subln[0], qk_lat, vg_lat,
                       [(qk_ctx, vg_ctx, n_ctx), (qk_lat, vg_lat, seq)],
                       batch=bsz, n_q=seq, tq=512, lam_init=lam_init)
    a_ctx = _attention(attn_lam[0], attn_subln[0], qk_ctx, vg_ctx,
                       [(qk_ctx, vg_ctx, n_ctx)],
                       batch=bsz, n_q=n_ctx, tq=n_ctx, lam_init=lam_init)
    w_out = attn_w_out[0].astype(BF16)
    o_lat = _matmul(a_lat, w_out)
    o_ctx = _matmul(a_ctx, w_out)
    gate0 = gate
    shift, scale, gate = mods[1, 0], mods[1, 1], mods[1, 2]
    x_lat, h_lat = _resid_normmod(x_lat, o_lat, norm_post[0], gate0, norm_pre[1], scale, shift,
                                  tiles_per_batch=lat_tiles, bm=norm_bm)
    _, h_ctx = _resid_normmod(x_ctx, o_ctx, norm_post[0], gate0, norm_pre[1], scale, shift,
                              fixed_row=ctx_row, bm=norm_bm)
    w_in = s5_w_in[0].astype(BF16)
    p_lat = _matmul(h_lat, w_in)
    u_ctx = _matmul(h_ctx, w_in[:, :d])

    tables = _s5_tables(s5_A_re[0], s5_A_im[0], s5_log_dt[0], s5_B_re[0], s5_B_im[0],
                        s5_C_re[0], s5_C_im[0], s5_D[0])
    yact = _s5_mix(u_ctx, p_lat, *tables, batch=bsz, n_ctx=n_ctx, seq=seq)
    gl = _matmul_glu(yact, s5_w_glu[0].astype(BF16), p_lat)
    o_lat = _matmul(gl, s5_w_out[0].astype(BF16))
    x_lat = _resid(x_lat, o_lat, norm_post[1], gate, tiles_per_batch=lat_tiles, bm=norm_bm)
    return x_lat.reshape(bsz, seq, d)
```

```python
import functools
import math

import jax
import jax.numpy as jnp
from jax import lax
from jax.experimental import pallas as pl
from jax.experimental.pallas import tpu as pltpu

F32 = jnp.float32
BF16 = jnp.bfloat16

V7X_LANES = 128
V7X_SUBLANES = 8
V7X_VMEM_BYTES = 64 * 1024 * 1024
VMEM_LIMIT = V7X_VMEM_BYTES - 8 * 1024 * 1024

D_MODEL = 4096
GRID_W = 64
EPS = 1e-6
HEADS = 16
HEAD_DIM = 128
V_DIM = 2 * HEAD_DIM
QK_W = HEADS * 2 * HEAD_DIM
ROPE_BASE = 10000.0
S5_GROUP = 16
S5_GROUPS = D_MODEL // S5_GROUP
S5_STATE = 64
CHUNK = 16
CHUNK_W = CHUNK * S5_GROUP
MOD_ROWS = 8
_DN_LAST = (((1,), (1,)), ((), ()))


def _params(sem, vmem=VMEM_LIMIT):
    return pltpu.CompilerParams(dimension_semantics=sem, vmem_limit_bytes=vmem)


def _sigmoid(x):
    return 1.0 / (1.0 + jnp.exp(-x))


def _silu(x):
    return x * _sigmoid(x)


def _mod_kernel(c_ref, w_ref, b_ref, o_ref):
    a = _silu(c_ref[...])
    acc = jnp.dot(a, w_ref[0], preferred_element_type=F32, precision=lax.Precision.HIGHEST)
    o_ref[0] = acc + b_ref[0]


def _modulation(cc, ada_w, ada_b):
    depth, d, n = ada_w.shape
    bn = 1024
    return pl.pallas_call(
        _mod_kernel,
        grid=(depth, n // bn),
        in_specs=[
            pl.BlockSpec((MOD_ROWS, d), lambda i, j: (0, 0)),
            pl.BlockSpec((1, d, bn), lambda i, j: (i, 0, j)),
            pl.BlockSpec((1, 1, bn), lambda i, j: (i, 0, j)),
        ],
        out_specs=pl.BlockSpec((1, MOD_ROWS, bn), lambda i, j: (i, 0, j)),
        out_shape=jax.ShapeDtypeStruct((depth, MOD_ROWS, n), F32),
        compiler_params=_params(("arbitrary", "arbitrary")),
        name="modulation",
    )(cc, ada_w, ada_b.reshape(depth, 1, n))


def _rms(x):
    return x * lax.rsqrt(jnp.mean(x * x, axis=-1, keepdims=True) + EPS)


def _normmod_kernel(x_ref, g_ref, scale_ref, shift_ref, o_ref):
    y = _rms(x_ref[...]) * g_ref[...]
    o_ref[...] = (y * (1.0 + scale_ref[0]) + shift_ref[0]).astype(o_ref.dtype)


def _resid_kernel(x_ref, o_in_ref, g_ref, gate_ref, o_ref):
    y = _rms(o_in_ref[...].astype(F32)) * g_ref[...]
    o_ref[...] = x_ref[...] + gate_ref[0] * y


def _resid_normmod_kernel(x_ref, o_in_ref, g_post_ref, gate_ref, g_pre_ref, scale_ref, shift_ref, x_out_ref, h_ref):
    y = _rms(o_in_ref[...].astype(F32)) * g_post_ref[...]
    x = x_ref[...] + gate_ref[0] * y
    x_out_ref[...] = x
    h = _rms(x) * g_pre_ref[...]
    h_ref[...] = (h * (1.0 + scale_ref[0]) + shift_ref[0]).astype(h_ref.dtype)


def _mod_row_map(rows_per_mod_row, fixed_row):
    if fixed_row is not None:
        return lambda i: (fixed_row, 0, 0)
    return lambda i: (i // rows_per_mod_row, 0, 0)


def _normmod(x2, g, scale3, shift3, *, tiles_per_batch=None, fixed_row=None, bm=256):
    m, d = x2.shape
    row_map = _mod_row_map(tiles_per_batch, fixed_row)
    return pl.pallas_call(
        _normmod_kernel,
        grid=(m // bm,),
        in_specs=[
            pl.BlockSpec((bm, d), lambda i: (i, 0)),
            pl.BlockSpec((1, d), lambda i: (0, 0)),
            pl.BlockSpec((1, 1, d), row_map),
            pl.BlockSpec((1, 1, d), row_map),
        ],
        out_specs=pl.BlockSpec((bm, d), lambda i: (i, 0)),
        out_shape=jax.ShapeDtypeStruct((m, d), BF16),
        compiler_params=_params(("arbitrary",)),
        name="normmod",
    )(x2, g.reshape(1, d), scale3, shift3)


def _resid_normmod(x2, o2, g_post, gate3, g_pre, scale3, shift3, *, tiles_per_batch=None, fixed_row=None, bm=256):
    m, d = x2.shape
    row_map = _mod_row_map(tiles_per_batch, fixed_row)
    tile = pl.BlockSpec((bm, d), lambda i: (i, 0))
    vec = pl.BlockSpec((1, d), lambda i: (0, 0))
    mod = pl.BlockSpec((1, 1, d), row_map)
    return pl.pallas_call(
        _resid_normmod_kernel,
        grid=(m // bm,),
        in_specs=[tile, tile, vec, mod, vec, mod, mod],
        out_specs=[tile, tile],
        out_shape=[jax.ShapeDtypeStruct((m, d), F32), jax.ShapeDtypeStruct((m, d), BF16)],
        compiler_params=_params(("arbitrary",)),
        name="resid_normmod",
    )(x2, o2, g_post.reshape(1, d), gate3, g_pre.reshape(1, d), scale3, shift3)


def _resid(x2, o2, g, gate3, *, tiles_per_batch=None, fixed_row=None, bm=256):
    m, d = x2.shape
    row_map = _mod_row_map(tiles_per_batch, fixed_row)
    return pl.pallas_call(
        _resid_kernel,
        grid=(m // bm,),
        in_specs=[
            pl.BlockSpec((bm, d), lambda i: (i, 0)),
            pl.BlockSpec((bm, d), lambda i: (i, 0)),
            pl.BlockSpec((1, d), lambda i: (0, 0)),
            pl.BlockSpec((1, 1, d), row_map),
        ],
        out_specs=pl.BlockSpec((bm, d), lambda i: (i, 0)),
        out_shape=jax.ShapeDtypeStruct((m, d), F32),
        compiler_params=_params(("arbitrary",)),
        name="resid",
    )(x2, o2, g.reshape(1, d), gate3)


def _mm_kernel(x_ref, w_ref, o_ref):
    acc = jnp.dot(x_ref[...], w_ref[...], preferred_element_type=F32)
    o_ref[...] = acc.astype(o_ref.dtype)


def _mm_qk_kernel(x_ref, w_ref, cos_ref, sin_ref, o_ref, *, rope, q_tiles, q_scale):
    acc = jnp.dot(x_ref[...], w_ref[...], preferred_element_type=F32)
    sc = jnp.where(pl.program_id(1) < q_tiles, q_scale, 1.0).astype(F32)
    bn = acc.shape[1]
    if rope:
        cos = cos_ref[...]
        sin = sin_ref[...]
        lane = lax.broadcasted_iota(jnp.int32, (1, HEAD_DIM), 1)
        first = (lane % (HEAD_DIM // 2)) < (HEAD_DIM // 4)
    for c in range(bn // HEAD_DIM):
        xc = acc[:, c * HEAD_DIM:(c + 1) * HEAD_DIM]
        if rope:
            partner = jnp.where(first,
                                pltpu.roll(xc, HEAD_DIM - HEAD_DIM // 4, 1),
                                pltpu.roll(xc, HEAD_DIM // 4, 1))
            xc = xc * cos + partner * sin
        o_ref[:, c * HEAD_DIM:(c + 1) * HEAD_DIM] = (xc * sc).astype(o_ref.dtype)


def _mm_glu_kernel(x_ref, w_ref, y_ref, z_ref, o_ref):
    acc = jnp.dot(x_ref[...], w_ref[...], preferred_element_type=F32)
    y = y_ref[...].astype(F32)
    o_ref[...] = (y * _sigmoid(acc) * _silu(z_ref[...].astype(F32))).astype(o_ref.dtype)


def _mm_tiles(m, n):
    return min(m, 1024), min(n, 1024)


def _matmul(x, w, out_dtype=BF16):
    m, k = x.shape
    n = w.shape[1]
    bm, bn = _mm_tiles(m, n)
    return pl.pallas_call(
        _mm_kernel,
        grid=(m // bm, n // bn),
        in_specs=[pl.BlockSpec((bm, k), lambda i, j: (i, 0)),
                  pl.BlockSpec((k, bn), lambda i, j: (0, j))],
        out_specs=pl.BlockSpec((bm, bn), lambda i, j: (i, j)),
        out_shape=jax.ShapeDtypeStruct((m, n), out_dtype),
        compiler_params=_params(("arbitrary", "arbitrary")),
        name="proj",
    )(x, w)


def _matmul_qk(x, w, cos_t, sin_t, *, rope):
    m, k = x.shape
    n = w.shape[1]
    bm, bn = _mm_tiles(m, n)
    tiles_per_seq = cos_t.shape[0] // bm if rope else 1
    tab_map = (lambda i, j: (i % tiles_per_seq, 0)) if rope else (lambda i, j: (0, 0))
    tab_rows = bm if rope else V7X_SUBLANES
    kern = functools.partial(_mm_qk_kernel, rope=rope, q_tiles=(n // 2) // bn,
                             q_scale=HEAD_DIM ** -0.5 * math.log2(math.e))
    return pl.pallas_call(
        kern,
        grid=(m // bm, n // bn),
        in_specs=[pl.BlockSpec((bm, k), lambda i, j: (i, 0)),
                  pl.BlockSpec((k, bn), lambda i, j: (0, j)),
                  pl.BlockSpec((tab_rows, HEAD_DIM), tab_map),
                  pl.BlockSpec((tab_rows, HEAD_DIM), tab_map)],
        out_specs=pl.BlockSpec((bm, bn), lambda i, j: (i, j)),
        out_shape=jax.ShapeDtypeStruct((m, n), BF16),
        compiler_params=_params(("arbitrary", "arbitrary")),
        name="proj_qk_rope" if rope else "proj_qk",
    )(x, w, cos_t, sin_t)


def _matmul_glu(y, w, p):
    m, k = y.shape
    n = w.shape[1]
    bm, bn = _mm_tiles(m, n)
    z_off = n // bn
    return pl.pallas_call(
        _mm_glu_kernel,
        grid=(m // bm, n // bn),
        in_specs=[pl.BlockSpec((bm, k), lambda i, j: (i, 0)),
                  pl.BlockSpec((k, bn), lambda i, j: (0, j)),
                  pl.BlockSpec((bm, bn), lambda i, j: (i, j)),
                  pl.BlockSpec((bm, bn), lambda i, j: (i, z_off + j))],
        out_specs=pl.BlockSpec((bm, bn), lambda i, j: (i, j)),
        out_shape=jax.ShapeDtypeStruct((m, n), BF16),
        compiler_params=_params(("arbitrary", "arbitrary")),
        name="proj_glu",
    )(y, w, y, p)


ATT_KEY_CHUNK = 256
ATT_PV_GROUP = 1
ATT_LAG = 4


def _lane_block_reduce(op, x):
    return functools.reduce(op, [x[:, V7X_LANES * j:V7X_LANES * (j + 1)] for j in range(x.shape[1] // V7X_LANES)])


def _attn_kernel(lam_ref, g_ref, q_ref, gate_ref, *refs, n_seg, lam_init):
    o_ref, s_scr = refs[2 * n_seg:]
    segs = [(refs[2 * i], refs[2 * i + 1]) for i in range(n_seg)]
    lv = lam_ref[...]
    t1 = jnp.sum(lv[0:1] * lv[1:2], axis=1, keepdims=True)
    t2 = jnp.sum(lv[2:3] * lv[3:4], axis=1, keepdims=True)
    lam = jnp.exp(t1) - jnp.exp(t2) + lam_init

    ch = s_scr.shape[3]
    chunks = []
    for si, (k, _) in enumerate(segs):
        for start in range(0, k.shape[0], ch):
            chunks.append((si, start, min(ch, k.shape[0] - start), len(chunks)))
    n_main = segs[-1][0].shape[0] // ch
    n_head = len(chunks) - n_main

    q = [q_ref[:, :HEAD_DIM], q_ref[:, HEAD_DIM:]]
    mrun = [None, None]
    for si, start, size, c in chunks:
        k_ref = segs[si][0]
        for n in range(2):
            s = lax.dot_general(q[n], k_ref[start:start + size, HEAD_DIM * n:HEAD_DIM * (n + 1)],
                                _DN_LAST, preferred_element_type=F32)
            s_scr[n, c, :, :size] = s
            part = _lane_block_reduce(jnp.maximum, s)
            mrun[n] = part if mrun[n] is None else jnp.maximum(mrun[n], part)
    m = [jnp.max(mrun[n], axis=1, keepdims=True) for n in range(2)]

    tq = q_ref.shape[0]
    m_bits = [pltpu.bitcast(jnp.broadcast_to(m[n], (tq, V7X_LANES)), jnp.uint32) for n in range(2)]
    lrun = [None, None]
    acc = [None, None]
    hist = [[], []]
    groups = []
    for si in range(n_seg):
        own = [ck for ck in chunks if ck[0] == si]
        groups += [own[i:i + ATT_PV_GROUP] for i in range(0, len(own), ATT_PV_GROUP)]
    for idx, grp in enumerate(groups):
        si, start = grp[0][0], grp[0][1]
        v = segs[si][1][start:start + sum(ck[2] for ck in grp), :]
        for n in range(2):
            mb = m_bits[n]
            if idx >= ATT_LAG:
                mb = (mb.reshape(tq // V7X_SUBLANES, V7X_SUBLANES, V7X_LANES) | hist[n][idx - ATT_LAG][None])
                mb = mb.reshape(tq, V7X_LANES)
            mc = pltpu.bitcast(mb, F32)
            ps = [jnp.exp2(s_scr[n, c, :, V7X_LANES * j:V7X_LANES * (j + 1)] - mc)
                  for _, _, size, c in grp for j in range(size // V7X_LANES)]
            part = functools.reduce(jnp.add, ps)
            lrun[n] = part if lrun[n] is None else lrun[n] + part
            pv = jnp.dot(jnp.concatenate(ps, axis=1).astype(BF16), v, preferred_element_type=F32)
            acc[n] = pv if acc[n] is None else acc[n] + pv
            tail = pltpu.bitcast(pv[tq - V7X_SUBLANES:, V_DIM - V7X_LANES:], jnp.uint32)
            hist[n].append((tail >> 16) >> 16)
    l1 = jnp.sum(lrun[0], axis=1, keepdims=True)
    l2 = jnp.sum(lrun[1], axis=1, keepdims=True)
    o = acc[0] * (1.0 / l1) - acc[1] * (lam / l2)
    o = _rms(o) * g_ref[...] * (1.0 - lam_init)
    o_ref[...] = (o * _silu(gate_ref[...].astype(F32))).astype(o_ref.dtype)


def _attention(lam_vecs, subln_g, q_arr, gate_arr, segs, *, batch, n_q, tq, lam_init):
    nq_tiles = n_q // tq
    n_seg = len(segs)
    in_specs = [
        pl.BlockSpec((4, HEAD_DIM), lambda b, h, i: (0, 0)),
        pl.BlockSpec((1, V_DIM), lambda b, h, i: (0, 0)),
        pl.BlockSpec((tq, V_DIM), lambda b, h, i: (b * nq_tiles + i, h)),
        pl.BlockSpec((tq, V_DIM), lambda b, h, i: (b * nq_tiles + i, HEADS + h)),
    ]
    args = [lam_vecs, subln_g.reshape(1, V_DIM), q_arr, gate_arr]
    for k_arr, v_arr, rows in segs:
        in_specs.append(pl.BlockSpec((rows, V_DIM), lambda b, h, i: (b, HEADS + h)))
        in_specs.append(pl.BlockSpec((rows, V_DIM), lambda b, h, i: (b, h)))
        args += [k_arr, v_arr]
    kern = functools.partial(_attn_kernel, n_seg=n_seg, lam_init=lam_init)
    ch = min(ATT_KEY_CHUNK, max(rows for _, _, rows in segs))
    n_chunks = sum(pl.cdiv(rows, ch) for _, _, rows in segs)
    return pl.pallas_call(
        kern,
        grid=(batch, HEADS, nq_tiles),
        in_specs=in_specs,
        out_specs=pl.BlockSpec((tq, V_DIM), lambda b, h, i: (b * nq_tiles + i, h)),
        out_shape=jax.ShapeDtypeStruct((batch * n_q, HEADS * V_DIM), BF16),
        scratch_shapes=[pltpu.VMEM((2, n_chunks, tq, ch), F32)],
        compiler_params=_params(("arbitrary", "arbitrary", "arbitrary")),
        name="diff_attn",
    )(*args)


S5_BLOCK_GROUPS = V7X_LANES // S5_GROUP
S5_PASS_GROUPS = 4
S5_ROW_CHUNK = 32


def _shift_lanes_256(x, n, lane):
    a, b = x[:, :V7X_LANES], x[:, V7X_LANES:]
    zero = jnp.zeros_like(a)
    if n == 0:
        return x
    if n > 0:
        q, r = divmod(n, V7X_LANES)
        if r == 0:
            lo, hi = zero, a
        else:
            ra, rb = pltpu.roll(a, r, 1), pltpu.roll(b, r, 1)
            keep = lane >= r
            if q == 0:
                lo, hi = jnp.where(keep, ra, 0.0), jnp.where(keep, rb, ra)
            else:
                lo, hi = zero, jnp.where(keep, ra, 0.0)
    else:
        q, r = divmod(-n, V7X_LANES)
        if r == 0:
            lo, hi = b, zero
        else:
            ra, rb = pltpu.roll(a, V7X_LANES - r, 1), pltpu.roll(b, V7X_LANES - r, 1)
            keep = lane < V7X_LANES - r
            if q == 0:
                lo, hi = jnp.where(keep, ra, rb), jnp.where(keep, rb, 0.0)
            else:
                lo, hi = jnp.where(keep, rb, 0.0), zero
    return jnp.concatenate([lo, hi], axis=1)


def _token_slot(t):
    return (t % 2) * (CHUNK // 2) + t // 2


def _transpose_lane_blocks(vs, lane):
    vs = list(vs)
    d = len(vs) // 2
    while d >= 1:
        low = (lane // (S5_GROUP * d)) % 2 == 0
        for i in range(len(vs)):
            if i & d == 0:
                x, y = vs[i], vs[i + d]
                vs[i] = jnp.where(low, x, pltpu.roll(y, S5_GROUP * d, 1))
                vs[i + d] = jnp.where(low, pltpu.roll(x, V7X_LANES - S5_GROUP * d, 1), y)
        d //= 2
    return vs


def _s5_state_rows(batch, n_ctx_chunks, n_lat_chunks):
    ctx_pitch = n_ctx_chunks + V7X_SUBLANES
    lat_pitch = n_lat_chunks + V7X_SUBLANES
    lat_base = batch * ctx_pitch
    return ctx_pitch, lat_pitch, lat_base, lat_base + batch * lat_pitch


def _gelu_tanh(y):
    inner = math.sqrt(2.0 / math.pi) * (y + 0.044715 * (y * y * y))
    return 0.5 * y * (1.0 + jnp.tanh(inner))


def _s5_kernel(uctx_ref, ulat_ref, dtab_ref, st_ref, rt_ref, kc_ref, bt_ref, coef_ref, pm_ref, o_ref,
               w_scr, u_scr, s_scr, y_scr, m_scr, mt_scr, *, batch, n_ctx_chunks, n_lat_chunks):
    ng = S5_BLOCK_GROUPS
    rc = S5_ROW_CHUNK
    pairs = CHUNK // 2
    ctx_rows = batch * n_ctx_chunks
    lat_rows = batch * n_lat_chunks
    ctx_pitch, lat_pitch, lat_base = _s5_state_rows(batch, n_ctx_chunks, n_lat_chunks)[:3]
    lane = lax.broadcasted_iota(jnp.int32, (1, V7X_LANES), 1)

    w_scr[0:ctx_rows * pairs, :] = pltpu.bitcast(uctx_ref[...], jnp.uint32)
    w_scr[ctx_rows * pairs:, :] = pltpu.bitcast(ulat_ref[...], jnp.uint32)

    def gather(i, carry):
        r0 = pl.multiple_of(i * rc, rc)
        vs = [w_scr[pl.ds(r0 * pairs + k, rc, stride=pairs), :] for k in range(pairs)]
        for g, x in enumerate(_transpose_lane_blocks(vs, lane)):
            for par, bits in enumerate((x << 16, x & jnp.uint32(0xFFFF0000))):
                tok = lax.bitcast_convert_type(bits, F32)
                u_scr[g, pl.ds(r0, rc), V7X_LANES * par:V7X_LANES * (par + 1)] = tok.astype(BF16)
        return carry

    lax.fori_loop(0, (ctx_rows + lat_rows) // rc, gather, 0, unroll=2)

    for g in range(ng):
        kf = lax.dot_general(bt_ref[g, 0], kc_ref[g, 0], _DN_LAST, preferred_element_type=F32,
                             precision=lax.Precision.HIGHEST)
        kb = lax.dot_general(bt_ref[g, 1], kc_ref[g, 1], _DN_LAST, preferred_element_type=F32,
                             precision=lax.Precision.HIGHEST)
        for s in range(CHUNK):
            blk = (_shift_lanes_256(kf, S5_GROUP * s, lane)
                   + _shift_lanes_256(kb, -S5_GROUP * (CHUNK - 1 - s), lane))
            pos = _token_slot(s)
            mt_scr[S5_GROUP * pos:S5_GROUP * (pos + 1), :] = blk.astype(BF16)
        m_scr[g] = jnp.dot(mt_scr[...], pm_ref[...], preferred_element_type=F32).astype(BF16)

    for p in range(ng // S5_PASS_GROUPS):
        g0 = p * S5_PASS_GROUPS
        for gi in range(S5_PASS_GROUPS):
            o1 = jnp.dot(u_scr[g0 + gi], st_ref[g0 + gi], preferred_element_type=F32)
            for cb in range(4):
                col = o1[:, V7X_LANES * cb:V7X_LANES * (cb + 1)]
                for b in range(batch):
                    s_scr[gi, cb, ctx_pitch * b:ctx_pitch * b + n_ctx_chunks, :] = (
                        col[n_ctx_chunks * b:n_ctx_chunks * (b + 1)])
                    s_scr[gi, cb, lat_base + lat_pitch * b:lat_base + lat_pitch * b + n_lat_chunks, :] = (
                        col[ctx_rows + n_lat_chunks * b:ctx_rows + n_lat_chunks * (b + 1)])

        coefs = [[coef_ref[g0 + gi, i:i + 1, :] for i in range(4)] for gi in range(S5_PASS_GROUPS)]

        def make_step(base, n, pitch):
            def step(i, carry):
                out = []
                for gi in range(S5_PASS_GROUPS):
                    hf, hsf, hb, hsb = carry[4 * gi:4 * gi + 4]
                    rf = pl.ds(base + i, batch, stride=pitch)
                    rb = pl.ds(base + n - 1 - i, batch, stride=pitch)
                    sf, ssf = s_scr.at[gi, 0][rf, :], s_scr.at[gi, 1][rf, :]
                    sb, ssb = s_scr.at[gi, 2][rb, :], s_scr.at[gi, 3][rb, :]
                    s_scr.at[gi, 0][rf, :] = hf
                    s_scr.at[gi, 2][rb, :] = hb
                    caf, cbf, cab, cbb = coefs[gi]
                    out += [caf * hf + cbf * hsf + sf, caf * hsf - cbf * hf + ssf,
                            cab * hb + cbb * hsb + sb, cab * hsb - cbb * hb + ssb]
                return tuple(out)
            return step

        carry = tuple(jnp.zeros((batch, V7X_LANES), F32) for _ in range(4 * S5_PASS_GROUPS))
        carry = lax.fori_loop(0, n_ctx_chunks, make_step(0, n_ctx_chunks, ctx_pitch), carry)
        lax.fori_loop(0, n_lat_chunks, make_step(lat_base, n_lat_chunks, lat_pitch), carry)

        def lat_states(gi, cb):
            return jnp.concatenate([s_scr[gi, cb, lat_base + lat_pitch * b:lat_base + lat_pitch * b + n_lat_chunks, :]
                                    for b in range(batch)], axis=0)

        for gi in range(S5_PASS_GROUPS):
            g = g0 + gi
            hin = jnp.concatenate([lat_states(gi, 0), lat_states(gi, 2)], axis=1)
            y = jnp.dot(u_scr[g, ctx_rows:, :], m_scr[g], preferred_element_type=F32)
            y = y + lax.dot_general(hin.astype(BF16), rt_ref[g], _DN_LAST, preferred_element_type=F32)
            y = _gelu_tanh(y + dtab_ref[g] * u_scr[g, ctx_rows:, :].astype(F32))
            even = lax.bitcast_convert_type(y[:, :V7X_LANES].astype(BF16).astype(F32), jnp.uint32)
            odd = lax.bitcast_convert_type(y[:, V7X_LANES:].astype(BF16).astype(F32), jnp.uint32)
            y_scr[g] = (even >> 16) | odd

    def scatter(i, carry):
        r0 = pl.multiple_of(i * rc, rc)
        ys = [y_scr[g, pl.ds(r0, rc), :] for g in range(ng)]
        for k, x in enumerate(_transpose_lane_blocks(ys, lane)):
            w_scr[pl.ds((ctx_rows + r0) * pairs + k, rc, stride=pairs), :] = x
        return carry

    lax.fori_loop(0, lat_rows // rc, scatter, 0, unroll=2)
    o_ref[...] = pltpu.bitcast(w_scr[ctx_rows * pairs:, :], o_ref.dtype)


def _s5_mix(u_ctx, p_lat, st, rt, kc, bt, coef, dtab, pm, *, batch, n_ctx, seq):
    ng = S5_BLOCK_GROUPS
    width = u_ctx.shape[1]
    n_ctx_chunks, n_lat_chunks = n_ctx // CHUNK, seq // CHUNK
    rows = batch * (n_ctx_chunks + n_lat_chunks)
    lat_rows = batch * n_lat_chunks
    kern = functools.partial(_s5_kernel, batch=batch, n_ctx_chunks=n_ctx_chunks, n_lat_chunks=n_lat_chunks)
    return pl.pallas_call(
        kern,
        grid=(width // V7X_LANES,),
        in_specs=[
            pl.BlockSpec((batch * n_ctx, V7X_LANES), lambda i: (0, i)),
            pl.BlockSpec((batch * seq, V7X_LANES), lambda i: (0, i)),
            pl.BlockSpec((ng,) + dtab.shape[1:], lambda i: (i, 0, 0)),
            pl.BlockSpec((ng,) + st.shape[1:], lambda i: (i, 0, 0)),
            pl.BlockSpec((ng,) + rt.shape[1:], lambda i: (i, 0, 0)),
            pl.BlockSpec((ng,) + kc.shape[1:], lambda i: (i, 0, 0, 0)),
            pl.BlockSpec((ng,) + bt.shape[1:], lambda i: (i, 0, 0, 0)),
            pl.BlockSpec((ng,) + coef.shape[1:], lambda i: (i, 0, 0)),
            pl.BlockSpec(pm.shape, lambda i: (0, 0)),
        ],
        out_specs=pl.BlockSpec((batch * seq, V7X_LANES), lambda i: (0, i)),
        out_shape=jax.ShapeDtypeStruct((batch * seq, width), BF16),
        scratch_shapes=[
            pltpu.VMEM((rows * CHUNK // 2, V7X_LANES), jnp.uint32),
            pltpu.VMEM((ng, rows, CHUNK_W), BF16),
            pltpu.VMEM((S5_PASS_GROUPS, 4, _s5_state_rows(batch, n_ctx_chunks, n_lat_chunks)[3], V7X_LANES), F32),
            pltpu.VMEM((ng, lat_rows, V7X_LANES), jnp.uint32),
            pltpu.VMEM((ng, CHUNK_W, CHUNK_W), BF16),
            pltpu.VMEM((CHUNK_W, CHUNK_W), BF16),
        ],
        compiler_params=_params(("arbitrary",)),
        name="s5_mix",
    )(u_ctx, p_lat, dtab, st, rt, kc, bt, coef, pm)


def _s5_tables_kernel(ac_ref, ldt_ref, cc_ref, bc_ref, st_ref, rt_ref, kc_ref, bt_ref, coef_ref):
    gb = cc_ref.shape[0]
    half = 2 * S5_STATE
    lo = lax.broadcasted_iota(jnp.int32, (1, half), 1) < S5_STATE

    def swap(x):
        return pltpu.roll(x, S5_STATE, 1)

    def product(x, xs, y, ys):
        return x * y - xs * ys, x * ys + xs * y

    a_par = ac_ref[...]
    a_par_s = swap(a_par)
    dt = jnp.exp(ldt_ref[...])
    mag = jnp.exp(jnp.where(lo, a_par, a_par_s) * dt)
    ang = jnp.where(lo, a_par_s, a_par) * dt
    a1 = mag * jnp.where(lo, jnp.cos(ang), jnp.sin(ang))
    a1s = swap(a1)
    one = jnp.where(lo, 1.0, 0.0)
    sq = a_par * a_par
    den = sq + swap(sq)
    conj = jnp.where(lo, a_par, -a_par)
    re, im = product(a1 - one, swap(a1 - one), conj, swap(conj))
    f = jnp.where(lo, re, im) / den
    pw = [(jnp.broadcast_to(one, a1.shape), jnp.broadcast_to(swap(one), a1.shape))]
    for _ in range(CHUNK):
        re, im = product(pw[-1][0], pw[-1][1], a1, a1s)
        nxt = jnp.where(lo, re, im)
        pw.append((nxt, swap(nxt)))
    fs = swap(f)

    order = sorted(range(CHUNK), key=_token_slot)
    for g in range(gb):
        for d in range(2):
            r = 2 * g + d

            def row(v):
                return v[r:r + 1, :]

            c, b = cc_ref[g, d], bc_ref[g, d]
            cs, bs = swap(c), swap(b)
            re, im = product(b, bs, row(f), row(fs))
            bbar = jnp.where(lo, re, im)
            bbars = swap(bbar)
            bt_ref[g, d] = bbar
            for k in range(CHUNK + 1):
                re, im = product(c, cs, row(pw[k][0]), row(pw[k][1]))
                blk = jnp.where(lo, re, -im)
                if k < CHUNK:
                    kk = k if d == 0 else CHUNK - 1 - k
                    kc_ref[g, d, S5_GROUP * kk:S5_GROUP * (kk + 1), :] = blk
                t = k - 1 if d == 0 else CHUNK - k
                if 0 <= t < CHUNK:
                    slot = _token_slot(t)
                    rt_ref[g, S5_GROUP * slot:S5_GROUP * (slot + 1), half * d:half * (d + 1)] = blk.astype(BF16)
            for slot, s in enumerate(order):
                k = CHUNK - 1 - s if d == 0 else s
                re, im = product(bbar, bbars, row(pw[k][0]), row(pw[k][1]))
                rows_ = slice(S5_GROUP * slot, S5_GROUP * (slot + 1))
                st_ref[g, rows_, 2 * half * d:2 * half * d + half] = jnp.where(lo, re, im).astype(BF16)
                st_ref[g, rows_, 2 * half * d + half:2 * half * (d + 1)] = jnp.where(lo, im, -re).astype(BF16)
            p16, p16s = row(pw[CHUNK][0]), row(pw[CHUNK][1])
            coef_ref[g, 2 * d:2 * d + 1, :] = jnp.where(lo, p16, p16s)
            coef_ref[g, 2 * d + 1:2 * d + 2, :] = jnp.where(lo, -p16s, p16)


def _s5_tables(A_re, A_im, log_dt, B_re, B_im, C_re, C_im, d_skip):
    g = A_re.shape[1]
    gb = S5_BLOCK_GROUPS
    half = 2 * S5_STATE

    def rows_gd(x):
        return jnp.swapaxes(x, 0, 1)

    ac = rows_gd(jnp.concatenate([A_re, A_im], axis=-1)).reshape(2 * g, half)
    ldt = rows_gd(log_dt).reshape(2 * g, 1)
    cc = rows_gd(jnp.concatenate([C_re, C_im], axis=-1))
    bc = rows_gd(jnp.concatenate([B_re.transpose(0, 1, 3, 2), B_im.transpose(0, 1, 3, 2)], axis=-1))
    st, rt, kc, bt, coef = pl.pallas_call(
        _s5_tables_kernel,
        grid=(g // gb,),
        in_specs=[
            pl.BlockSpec((2 * gb, half), lambda i: (i, 0)),
            pl.BlockSpec((2 * gb, 1), lambda i: (i, 0)),
            pl.BlockSpec((gb, 2, S5_GROUP, half), lambda i: (i, 0, 0, 0)),
            pl.BlockSpec((gb, 2, S5_GROUP, half), lambda i: (i, 0, 0, 0)),
        ],
        out_specs=[
            pl.BlockSpec((gb, CHUNK_W, 4 * half), lambda i: (i, 0, 0)),
            pl.BlockSpec((gb, CHUNK_W, 2 * half), lambda i: (i, 0, 0)),
            pl.BlockSpec((gb, 2, CHUNK_W, half), lambda i: (i, 0, 0, 0)),
            pl.BlockSpec((gb, 2, S5_GROUP, half), lambda i: (i, 0, 0, 0)),
            pl.BlockSpec((gb, 4, half), lambda i: (i, 0, 0)),
        ],
        out_shape=[
            jax.ShapeDtypeStruct((g, CHUNK_W, 4 * half), BF16),
            jax.ShapeDtypeStruct((g, CHUNK_W, 2 * half), BF16),
            jax.ShapeDtypeStruct((g, 2, CHUNK_W, half), F32),
            jax.ShapeDtypeStruct((g, 2, S5_GROUP, half), F32),
            jax.ShapeDtypeStruct((g, 4, half), F32),
        ],
        compiler_params=_params(("arbitrary",)),
        name="s5_tables",
    )(ac, ldt, cc, bc)

    dtab = jnp.tile(d_skip.reshape(g, 1, S5_GROUP), (1, 1, CHUNK))
    col = jnp.arange(CHUNK_W)
    dst = jnp.array([_token_slot(t) for t in range(CHUNK)])[col // S5_GROUP] * S5_GROUP + col % S5_GROUP
    pm = (dst[:, None] == col[None, :]).astype(BF16)
    return st, rt, kc, bt, coef, dtab, pm


def _rope_tables(n_tokens):
    rows = n_tokens // GRID_W
    row = jnp.repeat(jnp.arange(rows), GRID_W).astype(F32)
    col = jnp.tile(jnp.arange(GRID_W), rows).astype(F32)
    n_freq = HEAD_DIM // 4
    inv_freq = ROPE_BASE ** (-jnp.arange(n_freq, dtype=F32) / n_freq)
    ar = row[:, None] * inv_freq
    ac = col[:, None] * inv_freq
    cos_t = jnp.concatenate([jnp.cos(ar), jnp.cos(ar), jnp.cos(ac), jnp.cos(ac)], axis=1)
    sin_t = jnp.concatenate([-jnp.sin(ar), jnp.sin(ar), -jnp.sin(ac), jnp.sin(ac)], axis=1)
    return cos_t, sin_t


def kernel(x, c, ctx, c_ctx, ada_w, ada_b, norm_pre, norm_post, attn_w_in, attn_w_out, attn_lam, attn_subln, s5_w_in, s5_A_re, s5_A_im, s5_log_dt, s5_B_re, s5_B_im, s5_C_re, s5_C_im, s5_D, s5_w_glu, s5_w_out):
    bsz, seq, d = x.shape
    n_ctx = ctx.shape[1]
    depth = ada_w.shape[0]
    assert d == D_MODEL and depth == 2 and bsz + 1 <= MOD_ROWS
    ctx_row = bsz

    cc = jnp.concatenate([c, c_ctx[None, :], jnp.zeros((MOD_ROWS - bsz - 1, d), F32)], axis=0)
    mods = _modulation(cc, ada_w, ada_b)
    mods = mods.reshape(depth, MOD_ROWS, 3, 1, d).transpose(0, 2, 1, 3, 4)

    x_lat = x.reshape(bsz * seq, d)
    x_ctx = ctx.reshape(bsz * n_ctx, d)
    norm_bm = 256
    lat_tiles = seq // norm_bm

    shift, scale, gate = mods[0, 0], mods[0, 1], mods[0, 2]
    h_lat = _normmod(x_lat, norm_pre[0], scale, shift, tiles_per_batch=lat_tiles, bm=norm_bm)
    h_ctx = _normmod(x_ctx, norm_pre[0], scale, shift, fixed_row=ctx_row, bm=norm_bm)

    w_in = attn_w_in[0].astype(BF16)
    w_qk, w_vg = w_in[:, :2 * QK_W], w_in[:, 2 * QK_W:]
    cos_t, sin_t = _rope_tables(seq)
    qk_lat = _matmul_qk(h_lat, w_qk, cos_t, sin_t, rope=True)
    vg_lat = _matmul(h_lat, w_vg)
    dummy = jnp.zeros((V7X_SUBLANES, HEAD_DIM), F32)
    qk_ctx = _matmul_qk(h_ctx, w_qk, dummy, dummy, rope=False)
    vg_ctx = _matmul(h_ctx, w_vg)

    lam_init = 0.8 - 0.6 * math.exp(-0.3 * 0)
    a_lat = _attention(attn_lam[0], attn_subln[0], qk_lat, vg_lat,
                       [(qk_ctx, vg_ctx, n_ctx), (qk_lat, vg_lat, seq)],
                       batch=bsz, n_q=seq, tq=512, lam_init=lam_init)
    a_ctx = _attention(attn_lam[0], attn_subln[0], qk_ctx, vg_ctx,
                       [(qk_ctx, vg_ctx, n_ctx)],
                       batch=bsz, n_q=n_ctx, tq=n_ctx, lam_init=lam_init)
    w_out = attn_w_out[0].astype(BF16)
    o_lat = _matmul(a_lat, w_out)
    o_ctx = _matmul(a_ctx, w_out)
    gate0 = gate
    shift, scale, gate = mods[1, 0], mods[1, 1], mods[1, 2]
    x_lat, h_lat = _resid_normmod(x_lat, o_lat, norm_post[0], gate0, norm_pre[1], scale, shift,
                                  tiles_per_batch=lat_tiles, bm=norm_bm)
    _, h_ctx = _resid_normmod(x_ctx, o_ctx, norm_post[0], gate0, norm_pre[1], scale, shift,
                              fixed_row=ctx_row, bm=norm_bm)
    w_in = s5_w_in[0].astype(BF16)
    p_lat = _matmul(h_lat, w_in)
    u_ctx = _matmul(h_ctx, w_in[:, :d])

    tables = _s5_tables(s5_A_re[0], s5_A_im[0], s5_log_dt[0], s5_B_re[0], s5_B_im[0],
                        s5_C_re[0], s5_C_im[0], s5_D[0])
    yact = _s5_mix(u_ctx, p_lat, *tables, batch=bsz, n_ctx=n_ctx, seq=seq)
    gl = _matmul_glu(yact, s5_w_glu[0].astype(BF16), p_lat)
    o_lat = _matmul(gl, s5_w_out[0].astype(BF16))
    x_lat = _resid(x_lat, o_lat, norm_post[1], gate, tiles_per_batch=lat_tiles, bm=norm_bm)
    return x_lat.reshape(bsz, seq, d)
```

```python
import functools
import math

import jax
import jax.numpy as jnp
from jax import lax
from jax.experimental import pallas as pl
from jax.experimental.pallas import tpu as pltpu

F32 = jnp.float32
BF16 = jnp.bfloat16

V7X_LANES = 128
V7X_SUBLANES = 8
V7X_VMEM_BYTES = 64 * 1024 * 1024
VMEM_LIMIT = V7X_VMEM_BYTES - 8 * 1024 * 1024

D_MODEL = 4096
GRID_W = 64
EPS = 1e-6
HEADS = 16
HEAD_DIM = 128
V_DIM = 2 * HEAD_DIM
QK_W = HEADS * 2 * HEAD_DIM
ROPE_BASE = 10000.0
S5_GROUP = 16
S5_GROUPS = D_MODEL // S5_GROUP
S5_STATE = 64
CHUNK = 16
CHUNK_W = CHUNK * S5_GROUP
MOD_ROWS = 8
_DN_LAST = (((1,), (1,)), ((), ()))


def _params(sem, vmem=VMEM_LIMIT):
    return pltpu.CompilerParams(dimension_semantics=sem, vmem_limit_bytes=vmem)


def _sigmoid(x):
    return 1.0 / (1.0 + jnp.exp(-x))


def _silu(x):
    return x * _sigmoid(x)


def _mod_kernel(c_ref, w_ref, b_ref, o_ref):
    a = _silu(c_ref[...]).astype(BF16)
    acc = jnp.dot(a, w_ref[0].astype(BF16), preferred_element_type=F32)
    o_ref[0] = acc + b_ref[0]


def _modulation(cc, ada_w, ada_b):
    depth, d, n = ada_w.shape
    bn = 1024
    return pl.pallas_call(
        _mod_kernel,
        grid=(depth, n // bn),
        in_specs=[
            pl.BlockSpec((MOD_ROWS, d), lambda i, j: (0, 0)),
            pl.BlockSpec((1, d, bn), lambda i, j: (i, 0, j)),
            pl.BlockSpec((1, 1, bn), lambda i, j: (i, 0, j)),
        ],
        out_specs=pl.BlockSpec((1, MOD_ROWS, bn), lambda i, j: (i, 0, j)),
        out_shape=jax.ShapeDtypeStruct((depth, MOD_ROWS, n), F32),
        compiler_params=_params(("arbitrary", "arbitrary")),
        name="modulation",
    )(cc, ada_w, ada_b.reshape(depth, 1, n))


def _rms(x):
    return x * lax.rsqrt(jnp.mean(x * x, axis=-1, keepdims=True) + EPS)


def _normmod_kernel(x_ref, g_ref, scale_ref, shift_ref, o_ref):
    y = _rms(x_ref[...]) * g_ref[...]
    o_ref[...] = (y * (1.0 + scale_ref[0]) + shift_ref[0]).astype(o_ref.dtype)


def _resid_kernel(x_ref, o_in_ref, g_ref, gate_ref, o_ref):
    y = _rms(o_in_ref[...].astype(F32)) * g_ref[...]
    o_ref[...] = x_ref[...] + gate_ref[0] * y


def _resid_normmod_kernel(x_ref, o_in_ref, g_post_ref, gate_ref, g_pre_ref, scale_ref, shift_ref, x_out_ref, h_ref):
    y = _rms(o_in_ref[...].astype(F32)) * g_post_ref[...]
    x = x_ref[...] + gate_ref[0] * y
    x_out_ref[...] = x
    h = _rms(x) * g_pre_ref[...]
    h_ref[...] = (h * (1.0 + scale_ref[0]) + shift_ref[0]).astype(h_ref.dtype)


def _mod_row_map(rows_per_mod_row, fixed_row):
    if fixed_row is not None:
        return lambda i: (fixed_row, 0, 0)
    return lambda i: (i // rows_per_mod_row, 0, 0)


def _normmod(x2, g, scale3, shift3, *, tiles_per_batch=None, fixed_row=None, bm=256):
    m, d = x2.shape
    row_map = _mod_row_map(tiles_per_batch, fixed_row)
    return pl.pallas_call(
        _normmod_kernel,
        grid=(m // bm,),
        in_specs=[
            pl.BlockSpec((bm, d), lambda i: (i, 0)),
            pl.BlockSpec((1, d), lambda i: (0, 0)),
            pl.BlockSpec((1, 1, d), row_map),
            pl.BlockSpec((1, 1, d), row_map),
        ],
        out_specs=pl.BlockSpec((bm, d), lambda i: (i, 0)),
        out_shape=jax.ShapeDtypeStruct((m, d), BF16),
        compiler_params=_params(("arbitrary",)),
        name="normmod",
    )(x2, g.reshape(1, d), scale3, shift3)


def _resid_normmod(x2, o2, g_post, gate3, g_pre, scale3, shift3, *, tiles_per_batch=None, fixed_row=None, bm=256):
    m, d = x2.shape
    row_map = _mod_row_map(tiles_per_batch, fixed_row)
    tile = pl.BlockSpec((bm, d), lambda i: (i, 0))
    vec = pl.BlockSpec((1, d), lambda i: (0, 0))
    mod = pl.BlockSpec((1, 1, d), row_map)
    return pl.pallas_call(
        _resid_normmod_kernel,
        grid=(m // bm,),
        in_specs=[tile, tile, vec, mod, vec, mod, mod],
        out_specs=[tile, tile],
        out_shape=[jax.ShapeDtypeStruct((m, d), F32), jax.ShapeDtypeStruct((m, d), BF16)],
        compiler_params=_params(("arbitrary",)),
        name="resid_normmod",
    )(x2, o2, g_post.reshape(1, d), gate3, g_pre.reshape(1, d), scale3, shift3)


def _resid(x2, o2, g, gate3, *, tiles_per_batch=None, fixed_row=None, bm=256):
    m, d = x2.shape
    row_map = _mod_row_map(tiles_per_batch, fixed_row)
    return pl.pallas_call(
        _resid_kernel,
        grid=(m // bm,),
        in_specs=[
            pl.BlockSpec((bm, d), lambda i: (i, 0)),
            pl.BlockSpec((bm, d), lambda i: (i, 0)),
            pl.BlockSpec((1, d), lambda i: (0, 0)),
            pl.BlockSpec((1, 1, d), row_map),
        ],
        out_specs=pl.BlockSpec((bm, d), lambda i: (i, 0)),
        out_shape=jax.ShapeDtypeStruct((m, d), F32),
        compiler_params=_params(("arbitrary",)),
        name="resid",
    )(x2, o2, g.reshape(1, d), gate3)


def _mm_kernel(x_ref, w_ref, o_ref):
    acc = jnp.dot(x_ref[...], w_ref[...], preferred_element_type=F32)
    o_ref[...] = acc.astype(o_ref.dtype)


def _mm_qk_kernel(x_ref, w_ref, cos_ref, sin_ref, o_ref, *, rope, q_tiles, q_scale):
    acc = jnp.dot(x_ref[...], w_ref[...], preferred_element_type=F32)
    sc = jnp.where(pl.program_id(1) < q_tiles, q_scale, 1.0).astype(F32)
    bn = acc.shape[1]
    if rope:
        cos = cos_ref[...]
        sin = sin_ref[...]
        lane = lax.broadcasted_iota(jnp.int32, (1, HEAD_DIM), 1)
        first = (lane % (HEAD_DIM // 2)) < (HEAD_DIM // 4)
    for c in range(bn // HEAD_DIM):
        xc = acc[:, c * HEAD_DIM:(c + 1) * HEAD_DIM]
        if rope:
            partner = jnp.where(first,
                                pltpu.roll(xc, HEAD_DIM - HEAD_DIM // 4, 1),
                                pltpu.roll(xc, HEAD_DIM // 4, 1))
            xc = xc * cos + partner * sin
        o_ref[:, c * HEAD_DIM:(c + 1) * HEAD_DIM] = (xc * sc).astype(o_ref.dtype)


def _mm_glu_kernel(x_ref, w_ref, y_ref, z_ref, o_ref):
    acc = jnp.dot(x_ref[...], w_ref[...], preferred_element_type=F32)
    y = y_ref[...].astype(F32)
    o_ref[...] = (y * _sigmoid(acc) * _silu(z_ref[...].astype(F32))).astype(o_ref.dtype)


def _mm_tiles(m, n):
    return min(m, 1024), min(n, 1024)


def _matmul(x, w, out_dtype=BF16, *, col0=0, n=None):
    m, k = x.shape
    n = w.shape[1] if n is None else n
    bm, bn = _mm_tiles(m, n)
    off = col0 // bn
    return pl.pallas_call(
        _mm_kernel,
        grid=(m // bm, n // bn),
        in_specs=[pl.BlockSpec((bm, k), lambda i, j: (i, 0)),
                  pl.BlockSpec((k, bn), lambda i, j: (0, off + j))],
        out_specs=pl.BlockSpec((bm, bn), lambda i, j: (i, j)),
        out_shape=jax.ShapeDtypeStruct((m, n), out_dtype),
        compiler_params=_params(("arbitrary", "arbitrary")),
        name="proj",
    )(x, w)


def _matmul_qk(x, w, cos_t, sin_t, *, rope):
    m, k = x.shape
    n = 2 * QK_W
    bm, bn = _mm_tiles(m, n)
    tiles_per_seq = cos_t.shape[0] // bm if rope else 1
    tab_map = (lambda i, j: (i % tiles_per_seq, 0)) if rope else (lambda i, j: (0, 0))
    tab_rows = bm if rope else V7X_SUBLANES
    kern = functools.partial(_mm_qk_kernel, rope=rope, q_tiles=(n // 2) // bn,
                             q_scale=HEAD_DIM ** -0.5 * math.log2(math.e))
    return pl.pallas_call(
        kern,
        grid=(m // bm, n // bn),
        in_specs=[pl.BlockSpec((bm, k), lambda i, j: (i, 0)),
                  pl.BlockSpec((k, bn), lambda i, j: (0, j)),
                  pl.BlockSpec((tab_rows, HEAD_DIM), tab_map),
                  pl.BlockSpec((tab_rows, HEAD_DIM), tab_map)],
        out_specs=pl.BlockSpec((bm, bn), lambda i, j: (i, j)),
        out_shape=jax.ShapeDtypeStruct((m, n), BF16),
        compiler_params=_params(("arbitrary", "arbitrary")),
        name="proj_qk_rope" if rope else "proj_qk",
    )(x, w, cos_t, sin_t)


def _matmul_glu(y, w, p):
    m, k = y.shape
    n = w.shape[1]
    bm, bn = _mm_tiles(m, n)
    z_off = n // bn
    return pl.pallas_call(
        _mm_glu_kernel,
        grid=(m // bm, n // bn),
        in_specs=[pl.BlockSpec((bm, k), lambda i, j: (i, 0)),
                  pl.BlockSpec((k, bn), lambda i, j: (0, j)),
                  pl.BlockSpec((bm, bn), lambda i, j: (i, j)),
                  pl.BlockSpec((bm, bn), lambda i, j: (i, z_off + j))],
        out_specs=pl.BlockSpec((bm, bn), lambda i, j: (i, j)),
        out_shape=jax.ShapeDtypeStruct((m, n), BF16),
        compiler_params=_params(("arbitrary", "arbitrary")),
        name="proj_glu",
    )(y, w, y, p)


ATT_KEY_CHUNK = 256
ATT_PV_GROUP = 1
ATT_LAG = 4


def _lane_block_reduce(op, x):
    return functools.reduce(op, [x[:, V7X_LANES * j:V7X_LANES * (j + 1)] for j in range(x.shape[1] // V7X_LANES)])


def _attn_kernel(lam_ref, g_ref, q_ref, gate_ref, *refs, n_seg, lam_init):
    o_ref, s_scr = refs[2 * n_seg:]
    segs = [(refs[2 * i], refs[2 * i + 1]) for i in range(n_seg)]
    lv = lam_ref[...]
    t1 = jnp.sum(lv[0:1] * lv[1:2], axis=1, keepdims=True)
    t2 = jnp.sum(lv[2:3] * lv[3:4], axis=1, keepdims=True)
    lam = jnp.exp(t1) - jnp.exp(t2) + lam_init

    ch = s_scr.shape[3]
    chunks = []
    for si, (k, _) in enumerate(segs):
        for start in range(0, k.shape[0], ch):
            chunks.append((si, start, min(ch, k.shape[0] - start), len(chunks)))
    n_main = segs[-1][0].shape[0] // ch
    n_head = len(chunks) - n_main

    q = [q_ref[:, :HEAD_DIM], q_ref[:, HEAD_DIM:]]
    mrun = [None, None]
    for si, start, size, c in chunks:
        k_ref = segs[si][0]
        for n in range(2):
            s = lax.dot_general(q[n], k_ref[start:start + size, HEAD_DIM * n:HEAD_DIM * (n + 1)],
                                _DN_LAST, preferred_element_type=F32)
            s_scr[n, c, :, :size] = s
            part = _lane_block_reduce(jnp.maximum, s)
            mrun[n] = part if mrun[n] is None else jnp.maximum(mrun[n], part)
    m = [jnp.max(mrun[n], axis=1, keepdims=True) for n in range(2)]

    tq = q_ref.shape[0]
    m_bits = [pltpu.bitcast(jnp.broadcast_to(m[n], (tq, V7X_LANES)), jnp.uint32) for n in range(2)]
    lrun = [None, None]
    acc = [None, None]
    hist = [[], []]
    groups = []
    for si in range(n_seg):
        own = [ck for ck in chunks if ck[0] == si]
        groups += [own[i:i + ATT_PV_GROUP] for i in range(0, len(own), ATT_PV_GROUP)]
    for idx, grp in enumerate(groups):
        si, start = grp[0][0], grp[0][1]
        v = segs[si][1][start:start + sum(ck[2] for ck in grp), :]
        for n in range(2):
            mb = m_bits[n]
            if idx >= ATT_LAG:
                mb = (mb.reshape(tq // V7X_SUBLANES, V7X_SUBLANES, V7X_LANES) | hist[n][idx - ATT_LAG][None])
                mb = mb.reshape(tq, V7X_LANES)
            mc = pltpu.bitcast(mb, F32)
            ps = [jnp.exp2(s_scr[n, c, :, V7X_LANES * j:V7X_LANES * (j + 1)] - mc)
                  for _, _, size, c in grp for j in range(size // V7X_LANES)]
            part = functools.reduce(jnp.add, ps)
            lrun[n] = part if lrun[n] is None else lrun[n] + part
            pv = jnp.dot(jnp.concatenate(ps, axis=1).astype(BF16), v, preferred_element_type=F32)
            acc[n] = pv if acc[n] is None else acc[n] + pv
            tail = pltpu.bitcast(pv[tq - V7X_SUBLANES:, V_DIM - V7X_LANES:], jnp.uint32)
            hist[n].append((tail >> 16) >> 16)
    l1 = jnp.sum(lrun[0], axis=1, keepdims=True)
    l2 = jnp.sum(lrun[1], axis=1, keepdims=True)
    o = acc[0] * (1.0 / l1) - acc[1] * (lam / l2)
    o = _rms(o) * g_ref[...] * (1.0 - lam_init)
    o_ref[...] = (o * _silu(gate_ref[...].astype(F32))).astype(o_ref.dtype)


def _attention(lam_vecs, subln_g, q_arr, gate_arr, segs, *, batch, n_q, tq, lam_init):
    nq_tiles = n_q // tq
    n_seg = len(segs)
    in_specs = [
        pl.BlockSpec((4, HEAD_DIM), lambda b, h, i: (0, 0)),
        pl.BlockSpec((1, V_DIM), lambda b, h, i: (0, 0)),
        pl.BlockSpec((tq, V_DIM), lambda b, h, i: (b * nq_tiles + i, h)),
        pl.BlockSpec((tq, V_DIM), lambda b, h, i: (b * nq_tiles + i, HEADS + h)),
    ]
    args = [lam_vecs, subln_g.reshape(1, V_DIM), q_arr, gate_arr]
    for k_arr, v_arr, rows in segs:
        in_specs.append(pl.BlockSpec((rows, V_DIM), lambda b, h, i: (b, HEADS + h)))
        in_specs.append(pl.BlockSpec((rows, V_DIM), lambda b, h, i: (b, h)))
        args += [k_arr, v_arr]
    kern = functools.partial(_attn_kernel, n_seg=n_seg, lam_init=lam_init)
    ch = min(ATT_KEY_CHUNK, max(rows for _, _, rows in segs))
    n_chunks = sum(pl.cdiv(rows, ch) for _, _, rows in segs)
    return pl.pallas_call(
        kern,
        grid=(batch, HEADS, nq_tiles),
        in_specs=in_specs,
        out_specs=pl.BlockSpec((tq, V_DIM), lambda b, h, i: (b * nq_tiles + i, h)),
        out_shape=jax.ShapeDtypeStruct((batch * n_q, HEADS * V_DIM), BF16),
        scratch_shapes=[pltpu.VMEM((2, n_chunks, tq, ch), F32)],
        compiler_params=_params(("arbitrary", "arbitrary", "arbitrary")),
        name="diff_attn",
    )(*args)


S5_BLOCK_GROUPS = V7X_LANES // S5_GROUP
S5_PASS_GROUPS = 4
S5_ROW_CHUNK = 32


def _shift_lanes_256(x, n, lane):
    a, b = x[:, :V7X_LANES], x[:, V7X_LANES:]
    zero = jnp.zeros_like(a)
    if n == 0:
        return x
    if n > 0:
        q, r = divmod(n, V7X_LANES)
        if r == 0:
            lo, hi = zero, a
        else:
            ra, rb = pltpu.roll(a, r, 1), pltpu.roll(b, r, 1)
            keep = lane >= r
            if q == 0:
                lo, hi = jnp.where(keep, ra, 0.0), jnp.where(keep, rb, ra)
            else:
                lo, hi = zero, jnp.where(keep, ra, 0.0)
    else:
        q, r = divmod(-n, V7X_LANES)
        if r == 0:
            lo, hi = b, zero
        else:
            ra, rb = pltpu.roll(a, V7X_LANES - r, 1), pltpu.roll(b, V7X_LANES - r, 1)
            keep = lane < V7X_LANES - r
            if q == 0:
                lo, hi = jnp.where(keep, ra, rb), jnp.where(keep, rb, 0.0)
            else:
                lo, hi = jnp.where(keep, rb, 0.0), zero
    return jnp.concatenate([lo, hi], axis=1)


def _token_slot(t):
    return (t % 2) * (CHUNK // 2) + t // 2


def _transpose_lane_blocks(vs, lane):
    vs = list(vs)
    d = len(vs) // 2
    while d >= 1:
        low = (lane // (S5_GROUP * d)) % 2 == 0
        for i in range(len(vs)):
            if i & d == 0:
                x, y = vs[i], vs[i + d]
                vs[i] = jnp.where(low, x, pltpu.roll(y, S5_GROUP * d, 1))
                vs[i + d] = jnp.where(low, pltpu.roll(x, V7X_LANES - S5_GROUP * d, 1), y)
        d //= 2
    return vs


def _s5_state_rows(batch, n_ctx_chunks, n_lat_chunks):
    ctx_pitch = n_ctx_chunks + V7X_SUBLANES
    lat_pitch = n_lat_chunks + V7X_SUBLANES
    lat_base = batch * ctx_pitch
    return ctx_pitch, lat_pitch, lat_base, lat_base + batch * lat_pitch


def _gelu_tanh(y):
    inner = math.sqrt(2.0 / math.pi) * (y + 0.044715 * (y * y * y))
    return 0.5 * y * (1.0 + jnp.tanh(inner))


def _s5_kernel(uctx_ref, ulat_ref, dtab_ref, st_ref, rt_ref, kc_ref, bt_ref, coef_ref, pm_ref, o_ref,
               w_scr, u_scr, s_scr, y_scr, m_scr, mt_scr, *, batch, n_ctx_chunks, n_lat_chunks):
    ng = S5_BLOCK_GROUPS
    rc = S5_ROW_CHUNK
    pairs = CHUNK // 2
    ctx_rows = batch * n_ctx_chunks
    lat_rows = batch * n_lat_chunks
    ctx_pitch, lat_pitch, lat_base = _s5_state_rows(batch, n_ctx_chunks, n_lat_chunks)[:3]
    lane = lax.broadcasted_iota(jnp.int32, (1, V7X_LANES), 1)

    w_scr[0:ctx_rows * pairs, :] = pltpu.bitcast(uctx_ref[...], jnp.uint32)
    w_scr[ctx_rows * pairs:, :] = pltpu.bitcast(ulat_ref[...], jnp.uint32)

    def gather(i, carry):
        r0 = pl.multiple_of(i * rc, rc)
        vs = [w_scr[pl.ds(r0 * pairs + k, rc, stride=pairs), :] for k in range(pairs)]
        for g, x in enumerate(_transpose_lane_blocks(vs, lane)):
            for par, bits in enumerate((x << 16, x & jnp.uint32(0xFFFF0000))):
                tok = lax.bitcast_convert_type(bits, F32)
                u_scr[g, pl.ds(r0, rc), V7X_LANES * par:V7X_LANES * (par + 1)] = tok.astype(BF16)
        return carry

    lax.fori_loop(0, (ctx_rows + lat_rows) // rc, gather, 0, unroll=2)

    for g in range(ng):
        kf = lax.dot_general(bt_ref[g, 0], kc_ref[g, 0], _DN_LAST, preferred_element_type=F32,
                             precision=lax.Precision.HIGHEST)
        kb = lax.dot_general(bt_ref[g, 1], kc_ref[g, 1], _DN_LAST, preferred_element_type=F32,
                             precision=lax.Precision.HIGHEST)
        for s in range(CHUNK):
            blk = (_shift_lanes_256(kf, S5_GROUP * s, lane)
                   + _shift_lanes_256(kb, -S5_GROUP * (CHUNK - 1 - s), lane))
            pos = _token_slot(s)
            mt_scr[S5_GROUP * pos:S5_GROUP * (pos + 1), :] = blk.astype(BF16)
        m_scr[g] = jnp.dot(mt_scr[...], pm_ref[...], preferred_element_type=F32).astype(BF16)

    for p in range(ng // S5_PASS_GROUPS):
        g0 = p * S5_PASS_GROUPS
        for gi in range(S5_PASS_GROUPS):
            o1 = jnp.dot(u_scr[g0 + gi], st_ref[g0 + gi], preferred_element_type=F32)
            for cb in range(4):
                col = o1[:, V7X_LANES * cb:V7X_LANES * (cb + 1)]
                for b in range(batch):
                    s_scr[gi, cb, ctx_pitch * b:ctx_pitch * b + n_ctx_chunks, :] = (
                        col[n_ctx_chunks * b:n_ctx_chunks * (b + 1)])
                    s_scr[gi, cb, lat_base + lat_pitch * b:lat_base + lat_pitch * b + n_lat_chunks, :] = (
                        col[ctx_rows + n_lat_chunks * b:ctx_rows + n_lat_chunks * (b + 1)])

        coefs = [[coef_ref[g0 + gi, i:i + 1, :] for i in range(4)] for gi in range(S5_PASS_GROUPS)]

        def make_step(base, n, pitch):
            def step(i, carry):
                out = []
                for gi in range(S5_PASS_GROUPS):
                    hf, hsf, hb, hsb = carry[4 * gi:4 * gi + 4]
                    rf = pl.ds(base + i, batch, stride=pitch)
                    rb = pl.ds(base + n - 1 - i, batch, stride=pitch)
                    sf, ssf = s_scr.at[gi, 0][rf, :], s_scr.at[gi, 1][rf, :]
                    sb, ssb = s_scr.at[gi, 2][rb, :], s_scr.at[gi, 3][rb, :]
                    s_scr.at[gi, 0][rf, :] = hf
                    s_scr.at[gi, 2][rb, :] = hb
                    caf, cbf, cab, cbb = coefs[gi]
                    out += [caf * hf + cbf * hsf + sf, caf * hsf - cbf * hf + ssf,
                            cab * hb + cbb * hsb + sb, cab * hsb - cbb * hb + ssb]
                return tuple(out)
            return step

        carry = tuple(jnp.zeros((batch, V7X_LANES), F32) for _ in range(4 * S5_PASS_GROUPS))
        carry = lax.fori_loop(0, n_ctx_chunks, make_step(0, n_ctx_chunks, ctx_pitch), carry)
        lax.fori_loop(0, n_lat_chunks, make_step(lat_base, n_lat_chunks, lat_pitch), carry)

        def lat_states(gi, cb):
            return jnp.concatenate([s_scr[gi, cb, lat_base + lat_pitch * b:lat_base + lat_pitch * b + n_lat_chunks, :]
                                    for b in range(batch)], axis=0)

        for gi in range(S5_PASS_GROUPS):
            g = g0 + gi
            hin = jnp.concatenate([lat_states(gi, 0), lat_states(gi, 2)], axis=1)
            y = jnp.dot(u_scr[g, ctx_rows:, :], m_scr[g], preferred_element_type=F32)
            y = y + lax.dot_general(hin.astype(BF16), rt_ref[g], _DN_LAST, preferred_element_type=F32)
            y = _gelu_tanh(y + dtab_ref[g] * u_scr[g, ctx_rows:, :].astype(F32))
            even = lax.bitcast_convert_type(y[:, :V7X_LANES].astype(BF16).astype(F32), jnp.uint32)
            odd = lax.bitcast_convert_type(y[:, V7X_LANES:].astype(BF16).astype(F32), jnp.uint32)
            y_scr[g] = (even >> 16) | odd

    def scatter(i, carry):
        r0 = pl.multiple_of(i * rc, rc)
        ys = [y_scr[g, pl.ds(r0, rc), :] for g in range(ng)]
        for k, x in enumerate(_transpose_lane_blocks(ys, lane)):
            w_scr[pl.ds((ctx_rows + r0) * pairs + k, rc, stride=pairs), :] = x
        return carry

    lax.fori_loop(0, lat_rows // rc, scatter, 0, unroll=2)
    o_ref[...] = pltpu.bitcast(w_scr[ctx_rows * pairs:, :], o_ref.dtype)


def _s5_mix(u_ctx, p_lat, st, rt, kc, bt, coef, dtab, pm, *, batch, n_ctx, seq):
    ng = S5_BLOCK_GROUPS
    width = u_ctx.shape[1]
    n_ctx_chunks, n_lat_chunks = n_ctx // CHUNK, seq // CHUNK
    rows = batch * (n_ctx_chunks + n_lat_chunks)
    lat_rows = batch * n_lat_chunks
    kern = functools.partial(_s5_kernel, batch=batch, n_ctx_chunks=n_ctx_chunks, n_lat_chunks=n_lat_chunks)
    return pl.pallas_call(
        kern,
        grid=(width // V7X_LANES,),
        in_specs=[
            pl.BlockSpec((batch * n_ctx, V7X_LANES), lambda i: (0, i)),
            pl.BlockSpec((batch * seq, V7X_LANES), lambda i: (0, i)),
            pl.BlockSpec((ng,) + dtab.shape[1:], lambda i: (i, 0, 0)),
            pl.BlockSpec((ng,) + st.shape[1:], lambda i: (i, 0, 0)),
            pl.BlockSpec((ng,) + rt.shape[1:], lambda i: (i, 0, 0)),
            pl.BlockSpec((ng,) + kc.shape[1:], lambda i: (i, 0, 0, 0)),
            pl.BlockSpec((ng,) + bt.shape[1:], lambda i: (i, 0, 0, 0)),
            pl.BlockSpec((ng,) + coef.shape[1:], lambda i: (i, 0, 0)),
            pl.BlockSpec(pm.shape, lambda i: (0, 0)),
        ],
        out_specs=pl.BlockSpec((batch * seq, V7X_LANES), lambda i: (0, i)),
        out_shape=jax.ShapeDtypeStruct((batch * seq, width), BF16),
        scratch_shapes=[
            pltpu.VMEM((rows * CHUNK // 2, V7X_LANES), jnp.uint32),
            pltpu.VMEM((ng, rows, CHUNK_W), BF16),
            pltpu.VMEM((S5_PASS_GROUPS, 4, _s5_state_rows(batch, n_ctx_chunks, n_lat_chunks)[3], V7X_LANES), F32),
            pltpu.VMEM((ng, lat_rows, V7X_LANES), jnp.uint32),
            pltpu.VMEM((ng, CHUNK_W, CHUNK_W), BF16),
            pltpu.VMEM((CHUNK_W, CHUNK_W), BF16),
        ],
        compiler_params=_params(("arbitrary",)),
        name="s5_mix",
    )(u_ctx, p_lat, dtab, st, rt, kc, bt, coef, pm)


def _s5_tables_kernel(ac_ref, ldt_ref, cc_ref, bc_ref, st_ref, rt_ref, kc_ref, bt_ref, coef_ref):
    gb = cc_ref.shape[0]
    half = 2 * S5_STATE
    lo = lax.broadcasted_iota(jnp.int32, (1, half), 1) < S5_STATE

    def swap(x):
        return pltpu.roll(x, S5_STATE, 1)

    def product(x, xs, y, ys):
        return x * y - xs * ys, x * ys + xs * y

    a_par = ac_ref[...]
    a_par_s = swap(a_par)
    dt = jnp.exp(ldt_ref[...])
    mag = jnp.exp(jnp.where(lo, a_par, a_par_s) * dt)
    ang = jnp.where(lo, a_par_s, a_par) * dt
    a1 = mag * jnp.where(lo, jnp.cos(ang), jnp.sin(ang))
    a1s = swap(a1)
    one = jnp.where(lo, 1.0, 0.0)
    sq = a_par * a_par
    den = sq + swap(sq)
    conj = jnp.where(lo, a_par, -a_par)
    re, im = product(a1 - one, swap(a1 - one), conj, swap(conj))
    f = jnp.where(lo, re, im) / den
    pw = [(jnp.broadcast_to(one, a1.shape), jnp.broadcast_to(swap(one), a1.shape))]
    for _ in range(CHUNK):
        re, im = product(pw[-1][0], pw[-1][1], a1, a1s)
        nxt = jnp.where(lo, re, im)
        pw.append((nxt, swap(nxt)))
    fs = swap(f)

    order = sorted(range(CHUNK), key=_token_slot)
    for g in range(gb):
        for d in range(2):
            r = 2 * g + d

            def row(v):
                return v[r:r + 1, :]

            c, b = cc_ref[g, d], bc_ref[g, d]
            cs, bs = swap(c), swap(b)
            re, im = product(b, bs, row(f), row(fs))
            bbar = jnp.where(lo, re, im)
            bbars = swap(bbar)
            bt_ref[g, d] = bbar
            for k in range(CHUNK + 1):
                re, im = product(c, cs, row(pw[k][0]), row(pw[k][1]))
                blk = jnp.where(lo, re, -im)
                if k < CHUNK:
                    kk = k if d == 0 else CHUNK - 1 - k
                    kc_ref[g, d, S5_GROUP * kk:S5_GROUP * (kk + 1), :] = blk
                t = k - 1 if d == 0 else CHUNK - k
                if 0 <= t < CHUNK:
                    slot = _token_slot(t)
                    rt_ref[g, S5_GROUP * slot:S5_GROUP * (slot + 1), half * d:half * (d + 1)] = blk.astype(BF16)
            for slot, s in enumerate(order):
                k = CHUNK - 1 - s if d == 0 else s
                re, im = product(bbar, bbars, row(pw[k][0]), row(pw[k][1]))
                rows_ = slice(S5_GROUP * slot, S5_GROUP * (slot + 1))
                st_ref[g, rows_, 2 * half * d:2 * half * d + half] = jnp.where(lo, re, im).astype(BF16)
                st_ref[g, rows_, 2 * half * d + half:2 * half * (d + 1)] = jnp.where(lo, im, -re).astype(BF16)
            p16, p16s = row(pw[CHUNK][0]), row(pw[CHUNK][1])
            coef_ref[g, 2 * d:2 * d + 1, :] = jnp.where(lo, p16, p16s)
            coef_ref[g, 2 * d + 1:2 * d + 2, :] = jnp.where(lo, -p16s, p16)


def _s5_tables(A_re, A_im, log_dt, B_re, B_im, C_re, C_im, d_skip):
    g = A_re.shape[1]
    gb = S5_BLOCK_GROUPS
    half = 2 * S5_STATE

    def rows_gd(x):
        return jnp.swapaxes(x, 0, 1)

    ac = rows_gd(jnp.concatenate([A_re, A_im], axis=-1)).reshape(2 * g, half)
    ldt = rows_gd(log_dt).reshape(2 * g, 1)
    cc = rows_gd(jnp.concatenate([C_re, C_im], axis=-1))
    bc = rows_gd(jnp.concatenate([B_re.transpose(0, 1, 3, 2), B_im.transpose(0, 1, 3, 2)], axis=-1))
    st, rt, kc, bt, coef = pl.pallas_call(
        _s5_tables_kernel,
        grid=(g // gb,),
        in_specs=[
            pl.BlockSpec((2 * gb, half), lambda i: (i, 0)),
            pl.BlockSpec((2 * gb, 1), lambda i: (i, 0)),
            pl.BlockSpec((gb, 2, S5_GROUP, half), lambda i: (i, 0, 0, 0)),
            pl.BlockSpec((gb, 2, S5_GROUP, half), lambda i: (i, 0, 0, 0)),
        ],
        out_specs=[
            pl.BlockSpec((gb, CHUNK_W, 4 * half), lambda i: (i, 0, 0)),
            pl.BlockSpec((gb, CHUNK_W, 2 * half), lambda i: (i, 0, 0)),
            pl.BlockSpec((gb, 2, CHUNK_W, half), lambda i: (i, 0, 0, 0)),
            pl.BlockSpec((gb, 2, S5_GROUP, half), lambda i: (i, 0, 0, 0)),
            pl.BlockSpec((gb, 4, half), lambda i: (i, 0, 0)),
        ],
        out_shape=[
            jax.ShapeDtypeStruct((g, CHUNK_W, 4 * half), BF16),
            jax.ShapeDtypeStruct((g, CHUNK_W, 2 * half), BF16),
            jax.ShapeDtypeStruct((g, 2, CHUNK_W, half), F32),
            jax.ShapeDtypeStruct((g, 2, S5_GROUP, half), F32),
            jax.ShapeDtypeStruct((g, 4, half), F32),
        ],
        compiler_params=_params(("arbitrary",)),
        name="s5_tables",
    )(ac, ldt, cc, bc)

    dtab = jnp.tile(d_skip.reshape(g, 1, S5_GROUP), (1, 1, CHUNK))
    col = jnp.arange(CHUNK_W)
    dst = jnp.array([_token_slot(t) for t in range(CHUNK)])[col // S5_GROUP] * S5_GROUP + col % S5_GROUP
    pm = (dst[:, None] == col[None, :]).astype(BF16)
    return st, rt, kc, bt, coef, dtab, pm


def _rope_tables(n_tokens):
    rows = n_tokens // GRID_W
    row = jnp.repeat(jnp.arange(rows), GRID_W).astype(F32)
    col = jnp.tile(jnp.arange(GRID_W), rows).astype(F32)
    n_freq = HEAD_DIM // 4
    inv_freq = ROPE_BASE ** (-jnp.arange(n_freq, dtype=F32) / n_freq)
    ar = row[:, None] * inv_freq
    ac = col[:, None] * inv_freq
    cos_t = jnp.concatenate([jnp.cos(ar), jnp.cos(ar), jnp.cos(ac), jnp.cos(ac)], axis=1)
    sin_t = jnp.concatenate([-jnp.sin(ar), jnp.sin(ar), -jnp.sin(ac), jnp.sin(ac)], axis=1)
    return cos_t, sin_t


def kernel(x, c, ctx, c_ctx, ada_w, ada_b, norm_pre, norm_post, attn_w_in, attn_w_out, attn_lam, attn_subln, s5_w_in, s5_A_re, s5_A_im, s5_log_dt, s5_B_re, s5_B_im, s5_C_re, s5_C_im, s5_D, s5_w_glu, s5_w_out):
    bsz, seq, d = x.shape
    n_ctx = ctx.shape[1]
    depth = ada_w.shape[0]
    assert d == D_MODEL and depth == 2 and bsz + 1 <= MOD_ROWS
    ctx_row = bsz

    cc = jnp.concatenate([c, c_ctx[None, :], jnp.zeros((MOD_ROWS - bsz - 1, d), F32)], axis=0)
    mods = _modulation(cc, ada_w, ada_b)
    mods = mods.reshape(depth, MOD_ROWS, 3, 1, d).transpose(0, 2, 1, 3, 4)

    x_lat = x.reshape(bsz * seq, d)
    x_ctx = ctx.reshape(bsz * n_ctx, d)
    norm_bm = 256
    lat_tiles = seq // norm_bm

    shift, scale, gate = mods[0, 0], mods[0, 1], mods[0, 2]
    h_lat = _normmod(x_lat, norm_pre[0], scale, shift, tiles_per_batch=lat_tiles, bm=norm_bm)
    h_ctx = _normmod(x_ctx, norm_pre[0], scale, shift, fixed_row=ctx_row, bm=norm_bm)

    w_in = attn_w_in[0].astype(BF16)
    cos_t, sin_t = _rope_tables(seq)
    qk_lat = _matmul_qk(h_lat, w_in, cos_t, sin_t, rope=True)
    vg_lat = _matmul(h_lat, w_in, col0=2 * QK_W, n=2 * QK_W)
    dummy = jnp.zeros((V7X_SUBLANES, HEAD_DIM), F32)
    qk_ctx = _matmul_qk(h_ctx, w_in, dummy, dummy, rope=False)
    vg_ctx = _matmul(h_ctx, w_in, col0=2 * QK_W, n=2 * QK_W)

    lam_init = 0.8 - 0.6 * math.exp(-0.3 * 0)
    a_lat = _attention(attn_lam[0], attn_subln[0], qk_lat, vg_lat,
                       [(qk_ctx, vg_ctx, n_ctx), (qk_lat, vg_lat, seq)],
                       batch=bsz, n_q=seq, tq=512, lam_init=lam_init)
    a_ctx = _attention(attn_lam[0], attn_subln[0], qk_ctx, vg_ctx,
                       [(qk_ctx, vg_ctx, n_ctx)],
                       batch=bsz, n_q=n_ctx, tq=n_ctx, lam_init=lam_init)
    w_out = attn_w_out[0].astype(BF16)
    o_lat = _matmul(a_lat, w_out)
    o_ctx = _matmul(a_ctx, w_out)
    gate0 = gate
    shift, scale, gate = mods[1, 0], mods[1, 1], mods[1, 2]
    x_lat, h_lat = _resid_normmod(x_lat, o_lat, norm_post[0], gate0, norm_pre[1], scale, shift,
                                  tiles_per_batch=lat_tiles, bm=norm_bm)
    _, h_ctx = _resid_normmod(x_ctx, o_ctx, norm_post[0], gate0, norm_pre[1], scale, shift,
                              fixed_row=ctx_row, bm=norm_bm)
    w_in = s5_w_in[0].astype(BF16)
    p_lat = _matmul(h_lat, w_in)
    u_ctx = _matmul(h_ctx, w_in, n=d)

    tables = _s5_tables(s5_A_re[0], s5_A_im[0], s5_log_dt[0], s5_B_re[0], s5_B_im[0],
                        s5_C_re[0], s5_C_im[0], s5_D[0])
    yact = _s5_mix(u_ctx, p_lat, *tables, batch=bsz, n_ctx=n_ctx, seq=seq)
    gl = _matmul_glu(yact, s5_w_glu[0].astype(BF16), p_lat)
    o_lat = _matmul(gl, s5_w_out[0].astype(BF16))
    x_lat = _resid(x_lat, o_lat, norm_post[1], gate, tiles_per_batch=lat_tiles, bm=norm_bm)
    return x_lat.reshape(bsz, seq, d)
```

```python
import functools
import math

import jax
import jax.numpy as jnp
from jax import lax
from jax.experimental import pallas as pl
from jax.experimental.pallas import tpu as pltpu

F32 = jnp.float32
BF16 = jnp.bfloat16

V7X_LANES = 128
V7X_SUBLANES = 8
V7X_VMEM_BYTES = 64 * 1024 * 1024
VMEM_LIMIT = V7X_VMEM_BYTES - 8 * 1024 * 1024

D_MODEL = 4096
GRID_W = 64
EPS = 1e-6
HEADS = 16
HEAD_DIM = 128
V_DIM = 2 * HEAD_DIM
QK_W = HEADS * 2 * HEAD_DIM
ROPE_BASE = 10000.0
S5_GROUP = 16
S5_GROUPS = D_MODEL // S5_GROUP
S5_STATE = 64
CHUNK = 16
CHUNK_W = CHUNK * S5_GROUP
MOD_ROWS = 8
_DN_LAST = (((1,), (1,)), ((), ()))


def _params(sem, vmem=VMEM_LIMIT):
    return pltpu.CompilerParams(dimension_semantics=sem, vmem_limit_bytes=vmem)


def _sigmoid(x):
    return 0.5 * jnp.tanh(0.5 * x) + 0.5


def _silu(x):
    return x * _sigmoid(x)


def _mod_kernel(c_ref, w_ref, b_ref, o_ref):
    a = _silu(c_ref[...]).astype(BF16)
    acc = jnp.dot(a, w_ref[0].astype(BF16), preferred_element_type=F32)
    o_ref[0] = acc + b_ref[0]


def _modulation(cc, ada_w, ada_b):
    depth, d, n = ada_w.shape
    bn = 1024
    return pl.pallas_call(
        _mod_kernel,
        grid=(depth, n // bn),
        in_specs=[
            pl.BlockSpec((MOD_ROWS, d), lambda i, j: (0, 0)),
            pl.BlockSpec((1, d, bn), lambda i, j: (i, 0, j)),
            pl.BlockSpec((1, 1, bn), lambda i, j: (i, 0, j)),
        ],
        out_specs=pl.BlockSpec((1, MOD_ROWS, bn), lambda i, j: (i, 0, j)),
        out_shape=jax.ShapeDtypeStruct((depth, MOD_ROWS, n), F32),
        compiler_params=_params(("arbitrary", "arbitrary")),
        name="modulation",
    )(cc, ada_w, ada_b.reshape(depth, 1, n))


def _rms(x):
    return x * lax.rsqrt(jnp.mean(x * x, axis=-1, keepdims=True) + EPS)


def _normmod_kernel(x_ref, g_ref, scale_ref, shift_ref, o_ref):
    y = _rms(x_ref[...]) * g_ref[...]
    o_ref[...] = (y * (1.0 + scale_ref[0]) + shift_ref[0]).astype(o_ref.dtype)


def _resid_kernel(x_ref, o_in_ref, g_ref, gate_ref, o_ref):
    y = _rms(o_in_ref[...].astype(F32)) * g_ref[...]
    o_ref[...] = x_ref[...] + gate_ref[0] * y


def _resid_normmod_kernel(x_ref, o_in_ref, g_post_ref, gate_ref, g_pre_ref, scale_ref, shift_ref, x_out_ref, h_ref):
    y = _rms(o_in_ref[...].astype(F32)) * g_post_ref[...]
    x = x_ref[...] + gate_ref[0] * y
    x_out_ref[...] = x
    h = _rms(x) * g_pre_ref[...]
    h_ref[...] = (h * (1.0 + scale_ref[0]) + shift_ref[0]).astype(h_ref.dtype)


def _mod_row_map(rows_per_mod_row, fixed_row):
    if fixed_row is not None:
        return lambda i: (fixed_row, 0, 0)
    return lambda i: (i // rows_per_mod_row, 0, 0)


def _normmod(x2, g, scale3, shift3, *, tiles_per_batch=None, fixed_row=None, bm=256):
    m, d = x2.shape
    row_map = _mod_row_map(tiles_per_batch, fixed_row)
    return pl.pallas_call(
        _normmod_kernel,
        grid=(m // bm,),
        in_specs=[
            pl.BlockSpec((bm, d), lambda i: (i, 0)),
            pl.BlockSpec((1, d), lambda i: (0, 0)),
            pl.BlockSpec((1, 1, d), row_map),
            pl.BlockSpec((1, 1, d), row_map),
        ],
        out_specs=pl.BlockSpec((bm, d), lambda i: (i, 0)),
        out_shape=jax.ShapeDtypeStruct((m, d), BF16),
        compiler_params=_params(("arbitrary",)),
        name="normmod",
    )(x2, g.reshape(1, d), scale3, shift3)


def _resid_normmod(x2, o2, g_post, gate3, g_pre, scale3, shift3, *, tiles_per_batch=None, fixed_row=None, bm=256):
    m, d = x2.shape
    row_map = _mod_row_map(tiles_per_batch, fixed_row)
    tile = pl.BlockSpec((bm, d), lambda i: (i, 0))
    vec = pl.BlockSpec((1, d), lambda i: (0, 0))
    mod = pl.BlockSpec((1, 1, d), row_map)
    return pl.pallas_call(
        _resid_normmod_kernel,
        grid=(m // bm,),
        in_specs=[tile, tile, vec, mod, vec, mod, mod],
        out_specs=[tile, tile],
        out_shape=[jax.ShapeDtypeStruct((m, d), F32), jax.ShapeDtypeStruct((m, d), BF16)],
        compiler_params=_params(("arbitrary",)),
        name="resid_normmod",
    )(x2, o2, g_post.reshape(1, d), gate3, g_pre.reshape(1, d), scale3, shift3)


def _resid(x2, o2, g, gate3, *, tiles_per_batch=None, fixed_row=None, bm=256):
    m, d = x2.shape
    row_map = _mod_row_map(tiles_per_batch, fixed_row)
    return pl.pallas_call(
        _resid_kernel,
        grid=(m // bm,),
        in_specs=[
            pl.BlockSpec((bm, d), lambda i: (i, 0)),
            pl.BlockSpec((bm, d), lambda i: (i, 0)),
            pl.BlockSpec((1, d), lambda i: (0, 0)),
            pl.BlockSpec((1, 1, d), row_map),
        ],
        out_specs=pl.BlockSpec((bm, d), lambda i: (i, 0)),
        out_shape=jax.ShapeDtypeStruct((m, d), F32),
        compiler_params=_params(("arbitrary",)),
        name="resid",
    )(x2, o2, g.reshape(1, d), gate3)


def _mm_kernel(x_ref, w_ref, o_ref):
    acc = jnp.dot(x_ref[...], w_ref[...], preferred_element_type=F32)
    o_ref[...] = acc.astype(o_ref.dtype)


def _mm_qk_kernel(x_ref, w_ref, cos_ref, sin_ref, o_ref, *, rope, q_tiles, q_scale):
    acc = jnp.dot(x_ref[...], w_ref[...], preferred_element_type=F32)
    sc = jnp.where(pl.program_id(1) < q_tiles, q_scale, 1.0).astype(F32)
    bn = acc.shape[1]
    if rope:
        cos = cos_ref[...]
        sin = sin_ref[...]
        lane = lax.broadcasted_iota(jnp.int32, (1, HEAD_DIM), 1)
        first = (lane % (HEAD_DIM // 2)) < (HEAD_DIM // 4)
    for c in range(bn // HEAD_DIM):
        xc = acc[:, c * HEAD_DIM:(c + 1) * HEAD_DIM]
        if rope:
            partner = jnp.where(first,
                                pltpu.roll(xc, HEAD_DIM - HEAD_DIM // 4, 1),
                                pltpu.roll(xc, HEAD_DIM // 4, 1))
            xc = xc * cos + partner * sin
        o_ref[:, c * HEAD_DIM:(c + 1) * HEAD_DIM] = (xc * sc).astype(o_ref.dtype)


def _mm_glu_kernel(x_ref, w_ref, y_ref, z_ref, o_ref):
    acc = jnp.dot(x_ref[...], w_ref[...], preferred_element_type=F32)
    y = y_ref[...].astype(F32)
    o_ref[...] = (y * _sigmoid(acc) * _silu(z_ref[...].astype(F32))).astype(o_ref.dtype)


def _mm_tiles(m, n):
    return min(m, 1024), min(n, 1024)


def _matmul(x, w, out_dtype=BF16, *, col0=0, n=None):
    m, k = x.shape
    n = w.shape[1] if n is None else n
    bm, bn = _mm_tiles(m, n)
    off = col0 // bn
    return pl.pallas_call(
        _mm_kernel,
        grid=(m // bm, n // bn),
        in_specs=[pl.BlockSpec((bm, k), lambda i, j: (i, 0)),
                  pl.BlockSpec((k, bn), lambda i, j: (0, off + j))],
        out_specs=pl.BlockSpec((bm, bn), lambda i, j: (i, j)),
        out_shape=jax.ShapeDtypeStruct((m, n), out_dtype),
        compiler_params=_params(("arbitrary", "arbitrary")),
        name="proj",
    )(x, w)


def _matmul_qk(x, w, cos_t, sin_t, *, rope):
    m, k = x.shape
    n = 2 * QK_W
    bm, bn = _mm_tiles(m, n)
    tiles_per_seq = cos_t.shape[0] // bm if rope else 1
    tab_map = (lambda i, j: (i % tiles_per_seq, 0)) if rope else (lambda i, j: (0, 0))
    tab_rows = bm if rope else V7X_SUBLANES
    kern = functools.partial(_mm_qk_kernel, rope=rope, q_tiles=(n // 2) // bn,
                             q_scale=HEAD_DIM ** -0.5 * math.log2(math.e))
    return pl.pallas_call(
        kern,
        grid=(m // bm, n // bn),
        in_specs=[pl.BlockSpec((bm, k), lambda i, j: (i, 0)),
                  pl.BlockSpec((k, bn), lambda i, j: (0, j)),
                  pl.BlockSpec((tab_rows, HEAD_DIM), tab_map),
                  pl.BlockSpec((tab_rows, HEAD_DIM), tab_map)],
        out_specs=pl.BlockSpec((bm, bn), lambda i, j: (i, j)),
        out_shape=jax.ShapeDtypeStruct((m, n), BF16),
        compiler_params=_params(("arbitrary", "arbitrary")),
        name="proj_qk_rope" if rope else "proj_qk",
    )(x, w, cos_t, sin_t)


def _matmul_glu(y, w, p):
    m, k = y.shape
    n = w.shape[1]
    bm, bn = _mm_tiles(m, n)
    z_off = n // bn
    return pl.pallas_call(
        _mm_glu_kernel,
        grid=(m // bm, n // bn),
        in_specs=[pl.BlockSpec((bm, k), lambda i, j: (i, 0)),
                  pl.BlockSpec((k, bn), lambda i, j: (0, j)),
                  pl.BlockSpec((bm, bn), lambda i, j: (i, j)),
                  pl.BlockSpec((bm, bn), lambda i, j: (i, z_off + j))],
        out_specs=pl.BlockSpec((bm, bn), lambda i, j: (i, j)),
        out_shape=jax.ShapeDtypeStruct((m, n), BF16),
        compiler_params=_params(("arbitrary", "arbitrary")),
        name="proj_glu",
    )(y, w, y, p)


ATT_KEY_CHUNK = 256
ATT_PV_GROUP = 1
ATT_LAG = 4


def _lane_block_reduce(op, x):
    return functools.reduce(op, [x[:, V7X_LANES * j:V7X_LANES * (j + 1)] for j in range(x.shape[1] // V7X_LANES)])


def _attn_kernel(lam_ref, g_ref, q_ref, gate_ref, *refs, n_seg, lam_init):
    o_ref, s_scr = refs[2 * n_seg:]
    segs = [(refs[2 * i], refs[2 * i + 1]) for i in range(n_seg)]
    lv = lam_ref[...]
    t1 = jnp.sum(lv[0:1] * lv[1:2], axis=1, keepdims=True)
    t2 = jnp.sum(lv[2:3] * lv[3:4], axis=1, keepdims=True)
    lam = jnp.exp(t1) - jnp.exp(t2) + lam_init

    ch = s_scr.shape[3]
    chunks = []
    for si, (k, _) in enumerate(segs):
        for start in range(0, k.shape[0], ch):
            chunks.append((si, start, min(ch, k.shape[0] - start), len(chunks)))
    n_main = segs[-1][0].shape[0] // ch
    n_head = len(chunks) - n_main

    q = [q_ref[:, :HEAD_DIM], q_ref[:, HEAD_DIM:]]
    mrun = [None, None]
    for si, start, size, c in chunks:
        k_ref = segs[si][0]
        for n in range(2):
            s = lax.dot_general(q[n], k_ref[start:start + size, HEAD_DIM * n:HEAD_DIM * (n + 1)],
                                _DN_LAST, preferred_element_type=F32)
            s_scr[n, c, :, :size] = s
            part = _lane_block_reduce(jnp.maximum, s)
            mrun[n] = part if mrun[n] is None else jnp.maximum(mrun[n], part)
    m = [jnp.max(mrun[n], axis=1, keepdims=True) for n in range(2)]

    tq = q_ref.shape[0]
    m_bits = [pltpu.bitcast(jnp.broadcast_to(m[n], (tq, V7X_LANES)), jnp.uint32) for n in range(2)]
    lrun = [None, None]
    acc = [None, None]
    hist = [[], []]
    groups = []
    for si in range(n_seg):
        own = [ck for ck in chunks if ck[0] == si]
        groups += [own[i:i + ATT_PV_GROUP] for i in range(0, len(own), ATT_PV_GROUP)]
    for idx, grp in enumerate(groups):
        si, start = grp[0][0], grp[0][1]
        v = segs[si][1][start:start + sum(ck[2] for ck in grp), :]
        for n in range(2):
            mb = m_bits[n]
            if idx >= ATT_LAG:
                mb = (mb.reshape(tq // V7X_SUBLANES, V7X_SUBLANES, V7X_LANES) | hist[n][idx - ATT_LAG][None])
                mb = mb.reshape(tq, V7X_LANES)
            mc = pltpu.bitcast(mb, F32)
            ps = [jnp.exp2(s_scr[n, c, :, V7X_LANES * j:V7X_LANES * (j + 1)] - mc)
                  for _, _, size, c in grp for j in range(size // V7X_LANES)]
            part = functools.reduce(jnp.add, ps)
            lrun[n] = part if lrun[n] is None else lrun[n] + part
            per = ATT_KEY_CHUNK // V7X_LANES
            pv = functools.reduce(jnp.add, [
                jnp.dot(jnp.concatenate(ps[i:i + per], axis=1).astype(BF16),
                        v[ATT_KEY_CHUNK * (i // per):ATT_KEY_CHUNK * (i // per + 1), :],
                        preferred_element_type=F32)
                for i in range(0, len(ps), per)])
            acc[n] = pv if acc[n] is None else acc[n] + pv
            tail = pltpu.bitcast(pv[tq - V7X_SUBLANES:, V_DIM - V7X_LANES:], jnp.uint32)
            hist[n].append((tail >> 16) >> 16)
    l1 = jnp.sum(lrun[0], axis=1, keepdims=True)
    l2 = jnp.sum(lrun[1], axis=1, keepdims=True)
    o = acc[0] * (1.0 / l1) - acc[1] * (lam / l2)
    o = _rms(o) * g_ref[...] * (1.0 - lam_init)
    o_ref[...] = (o * _silu(gate_ref[...].astype(F32))).astype(o_ref.dtype)


def _attention(lam_vecs, subln_g, q_arr, gate_arr, segs, *, batch, n_q, tq, lam_init):
    nq_tiles = n_q // tq
    n_seg = len(segs)
    in_specs = [
        pl.BlockSpec((4, HEAD_DIM), lambda b, h, i: (0, 0)),
        pl.BlockSpec((1, V_DIM), lambda b, h, i: (0, 0)),
        pl.BlockSpec((tq, V_DIM), lambda b, h, i: (b * nq_tiles + i, h)),
        pl.BlockSpec((tq, V_DIM), lambda b, h, i: (b * nq_tiles + i, HEADS + h)),
    ]
    args = [lam_vecs, subln_g.reshape(1, V_DIM), q_arr, gate_arr]
    for k_arr, v_arr, rows in segs:
        in_specs.append(pl.BlockSpec((rows, V_DIM), lambda b, h, i: (b, HEADS + h)))
        in_specs.append(pl.BlockSpec((rows, V_DIM), lambda b, h, i: (b, h)))
        args += [k_arr, v_arr]
    kern = functools.partial(_attn_kernel, n_seg=n_seg, lam_init=lam_init)
    ch = min(ATT_KEY_CHUNK, max(rows for _, _, rows in segs))
    n_chunks = sum(pl.cdiv(rows, ch) for _, _, rows in segs)
    return pl.pallas_call(
        kern,
        grid=(batch, HEADS, nq_tiles),
        in_specs=in_specs,
        out_specs=pl.BlockSpec((tq, V_DIM), lambda b, h, i: (b * nq_tiles + i, h)),
        out_shape=jax.ShapeDtypeStruct((batch * n_q, HEADS * V_DIM), BF16),
        scratch_shapes=[pltpu.VMEM((2, n_chunks, tq, ch), F32)],
        compiler_params=_params(("arbitrary", "arbitrary", "arbitrary")),
        name="diff_attn",
    )(*args)


S5_BLOCK_GROUPS = V7X_LANES // S5_GROUP
S5_PASS_GROUPS = 4
S5_ROW_CHUNK = 32


def _shift_lanes_256(x, n, lane):
    a, b = x[:, :V7X_LANES], x[:, V7X_LANES:]
    zero = jnp.zeros_like(a)
    if n == 0:
        return x
    if n > 0:
        q, r = divmod(n, V7X_LANES)
        if r == 0:
            lo, hi = zero, a
        else:
            ra, rb = pltpu.roll(a, r, 1), pltpu.roll(b, r, 1)
            keep = lane >= r
            if q == 0:
                lo, hi = jnp.where(keep, ra, 0.0), jnp.where(keep, rb, ra)
            else:
                lo, hi = zero, jnp.where(keep, ra, 0.0)
    else:
        q, r = divmod(-n, V7X_LANES)
        if r == 0:
            lo, hi = b, zero
        else:
            ra, rb = pltpu.roll(a, V7X_LANES - r, 1), pltpu.roll(b, V7X_LANES - r, 1)
            keep = lane < V7X_LANES - r
            if q == 0:
                lo, hi = jnp.where(keep, ra, rb), jnp.where(keep, rb, 0.0)
            else:
                lo, hi = jnp.where(keep, rb, 0.0), zero
    return jnp.concatenate([lo, hi], axis=1)


def _token_slot(t):
    return (t % 2) * (CHUNK // 2) + t // 2


def _transpose_lane_blocks(vs, lane):
    vs = list(vs)
    d = len(vs) // 2
    while d >= 1:
        low = (lane // (S5_GROUP * d)) % 2 == 0
        for i in range(len(vs)):
            if i & d == 0:
                x, y = vs[i], vs[i + d]
                vs[i] = jnp.where(low, x, pltpu.roll(y, S5_GROUP * d, 1))
                vs[i + d] = jnp.where(low, pltpu.roll(x, V7X_LANES - S5_GROUP * d, 1), y)
        d //= 2
    return vs


def _s5_state_rows(batch, n_ctx_chunks, n_lat_chunks):
    ctx_pitch = n_ctx_chunks + V7X_SUBLANES
    lat_pitch = n_lat_chunks + V7X_SUBLANES
    lat_base = batch * ctx_pitch
    return ctx_pitch, lat_pitch, lat_base, lat_base + batch * lat_pitch


def _gelu_tanh(y):
    inner = math.sqrt(2.0 / math.pi) * (y + 0.044715 * (y * y * y))
    return 0.5 * y * (1.0 + jnp.tanh(inner))


def _s5_kernel(uctx_ref, ulat_ref, dtab_ref, st_ref, rt_ref, kc_ref, bt_ref, coef_ref, pm_ref, o_ref,
               w_scr, u_scr, s_scr, y_scr, m_scr, mt_scr, *, batch, n_ctx_chunks, n_lat_chunks):
    ng = S5_BLOCK_GROUPS
    rc = S5_ROW_CHUNK
    pairs = CHUNK // 2
    ctx_rows = batch * n_ctx_chunks
    lat_rows = batch * n_lat_chunks
    ctx_pitch, lat_pitch, lat_base = _s5_state_rows(batch, n_ctx_chunks, n_lat_chunks)[:3]
    lane = lax.broadcasted_iota(jnp.int32, (1, V7X_LANES), 1)

    w_scr[0:ctx_rows * pairs, :] = pltpu.bitcast(uctx_ref[...], jnp.uint32)
    w_scr[ctx_rows * pairs:, :] = pltpu.bitcast(ulat_ref[...], jnp.uint32)

    def gather(i, carry):
        r0 = pl.multiple_of(i * rc, rc)
        vs = [w_scr[pl.ds(r0 * pairs + k, rc, stride=pairs), :] for k in range(pairs)]
        for g, x in enumerate(_transpose_lane_blocks(vs, lane)):
            for par, bits in enumerate((x << 16, x & jnp.uint32(0xFFFF0000))):
                tok = lax.bitcast_convert_type(bits, F32)
                u_scr[g, pl.ds(r0, rc), V7X_LANES * par:V7X_LANES * (par + 1)] = tok.astype(BF16)
        return carry

    lax.fori_loop(0, (ctx_rows + lat_rows) // rc, gather, 0, unroll=2)

    for g in range(ng):
        kf = lax.dot_general(bt_ref[g, 0], kc_ref[g, 0], _DN_LAST, preferred_element_type=F32,
                             precision=lax.Precision.HIGHEST)
        kb = lax.dot_general(bt_ref[g, 1], kc_ref[g, 1], _DN_LAST, preferred_element_type=F32,
                             precision=lax.Precision.HIGHEST)
        for s in range(CHUNK):
            blk = (_shift_lanes_256(kf, S5_GROUP * s, lane)
                   + _shift_lanes_256(kb, -S5_GROUP * (CHUNK - 1 - s), lane))
            pos = _token_slot(s)
            mt_scr[S5_GROUP * pos:S5_GROUP * (pos + 1), :] = blk.astype(BF16)
        m_scr[g] = jnp.dot(mt_scr[...], pm_ref[...], preferred_element_type=F32).astype(BF16)

    for p in range(ng // S5_PASS_GROUPS):
        g0 = p * S5_PASS_GROUPS
        for gi in range(S5_PASS_GROUPS):
            o1 = jnp.dot(u_scr[g0 + gi], st_ref[g0 + gi], preferred_element_type=F32)
            for cb in range(4):
                col = o1[:, V7X_LANES * cb:V7X_LANES * (cb + 1)]
                for b in range(batch):
                    s_scr[gi, cb, ctx_pitch * b:ctx_pitch * b + n_ctx_chunks, :] = (
                        col[n_ctx_chunks * b:n_ctx_chunks * (b + 1)])
                    s_scr[gi, cb, lat_base + lat_pitch * b:lat_base + lat_pitch * b + n_lat_chunks, :] = (
                        col[ctx_rows + n_lat_chunks * b:ctx_rows + n_lat_chunks * (b + 1)])

        coefs = [[coef_ref[g0 + gi, i:i + 1, :] for i in range(4)] for gi in range(S5_PASS_GROUPS)]

        def make_step(base, n, pitch):
            def step(i, carry):
                out = []
                for gi in range(S5_PASS_GROUPS):
                    hf, hsf, hb, hsb = carry[4 * gi:4 * gi + 4]
                    rf = pl.ds(base + i, batch, stride=pitch)
                    rb = pl.ds(base + n - 1 - i, batch, stride=pitch)
                    sf, ssf = s_scr.at[gi, 0][rf, :], s_scr.at[gi, 1][rf, :]
                    sb, ssb = s_scr.at[gi, 2][rb, :], s_scr.at[gi, 3][rb, :]
                    s_scr.at[gi, 0][rf, :] = hf
                    s_scr.at[gi, 2][rb, :] = hb
                    caf, cbf, cab, cbb = coefs[gi]
                    out += [caf * hf + cbf * hsf + sf, caf * hsf - cbf * hf + ssf,
                            cab * hb + cbb * hsb + sb, cab * hsb - cbb * hb + ssb]
                return tuple(out)
            return step

        carry = tuple(jnp.zeros((batch, V7X_LANES), F32) for _ in range(4 * S5_PASS_GROUPS))
        carry = lax.fori_loop(0, n_ctx_chunks, make_step(0, n_ctx_chunks, ctx_pitch), carry)
        lax.fori_loop(0, n_lat_chunks, make_step(lat_base, n_lat_chunks, lat_pitch), carry)

        def lat_states(gi, cb):
            return jnp.concatenate([s_scr[gi, cb, lat_base + lat_pitch * b:lat_base + lat_pitch * b + n_lat_chunks, :]
                                    for b in range(batch)], axis=0)

        for gi in range(S5_PASS_GROUPS):
            g = g0 + gi
            hin = jnp.concatenate([lat_states(gi, 0), lat_states(gi, 2)], axis=1)
            y = jnp.dot(u_scr[g, ctx_rows:, :], m_scr[g], preferred_element_type=F32)
            y = y + lax.dot_general(hin.astype(BF16), rt_ref[g], _DN_LAST, preferred_element_type=F32)
            y = _gelu_tanh(y + dtab_ref[g] * u_scr[g, ctx_rows:, :].astype(F32))
            even = lax.bitcast_convert_type(y[:, :V7X_LANES].astype(BF16).astype(F32), jnp.uint32)
            odd = lax.bitcast_convert_type(y[:, V7X_LANES:].astype(BF16).astype(F32), jnp.uint32)
            y_scr[g] = (even >> 16) | odd

    def scatter(i, carry):
        r0 = pl.multiple_of(i * rc, rc)
        ys = [y_scr[g, pl.ds(r0, rc), :] for g in range(ng)]
        for k, x in enumerate(_transpose_lane_blocks(ys, lane)):
            w_scr[pl.ds((ctx_rows + r0) * pairs + k, rc, stride=pairs), :] = x
        return carry

    lax.fori_loop(0, lat_rows // rc, scatter, 0, unroll=2)
    o_ref[...] = pltpu.bitcast(w_scr[ctx_rows * pairs:, :], o_ref.dtype)


def _s5_mix(u_ctx, p_lat, st, rt, kc, bt, coef, dtab, pm, *, batch, n_ctx, seq):
    ng = S5_BLOCK_GROUPS
    width = u_ctx.shape[1]
    n_ctx_chunks, n_lat_chunks = n_ctx // CHUNK, seq // CHUNK
    rows = batch * (n_ctx_chunks + n_lat_chunks)
    lat_rows = batch * n_lat_chunks
    kern = functools.partial(_s5_kernel, batch=batch, n_ctx_chunks=n_ctx_chunks, n_lat_chunks=n_lat_chunks)
    return pl.pallas_call(
        kern,
        grid=(width // V7X_LANES,),
        in_specs=[
            pl.BlockSpec((batch * n_ctx, V7X_LANES), lambda i: (0, i)),
            pl.BlockSpec((batch * seq, V7X_LANES), lambda i: (0, i)),
            pl.BlockSpec((ng,) + dtab.shape[1:], lambda i: (i, 0, 0)),
            pl.BlockSpec((ng,) + st.shape[1:], lambda i: (i, 0, 0)),
            pl.BlockSpec((ng,) + rt.shape[1:], lambda i: (i, 0, 0)),
            pl.BlockSpec((ng,) + kc.shape[1:], lambda i: (i, 0, 0, 0)),
            pl.BlockSpec((ng,) + bt.shape[1:], lambda i: (i, 0, 0, 0)),
            pl.BlockSpec((ng,) + coef.shape[1:], lambda i: (i, 0, 0)),
            pl.BlockSpec(pm.shape, lambda i: (0, 0)),
        ],
        out_specs=pl.BlockSpec((batch * seq, V7X_LANES), lambda i: (0, i)),
        out_shape=jax.ShapeDtypeStruct((batch * seq, width), BF16),
        scratch_shapes=[
            pltpu.VMEM((rows * CHUNK // 2, V7X_LANES), jnp.uint32),
            pltpu.VMEM((ng, rows, CHUNK_W), BF16),
            pltpu.VMEM((S5_PASS_GROUPS, 4, _s5_state_rows(batch, n_ctx_chunks, n_lat_chunks)[3], V7X_LANES), F32),
            pltpu.VMEM((ng, lat_rows, V7X_LANES), jnp.uint32),
            pltpu.VMEM((ng, CHUNK_W, CHUNK_W), BF16),
            pltpu.VMEM((CHUNK_W, CHUNK_W), BF16),
        ],
        compiler_params=_params(("arbitrary",)),
        name="s5_mix",
    )(u_ctx, p_lat, dtab, st, rt, kc, bt, coef, pm)


def _s5_tables_kernel(ac_ref, ldt_ref, cc_ref, bc_ref, st_ref, rt_ref, kc_ref, bt_ref, coef_ref):
    gb = cc_ref.shape[0]
    half = 2 * S5_STATE
    lo = lax.broadcasted_iota(jnp.int32, (1, half), 1) < S5_STATE

    def swap(x):
        return pltpu.roll(x, S5_STATE, 1)

    def product(x, xs, y, ys):
        return x * y - xs * ys, x * ys + xs * y

    a_par = ac_ref[...]
    a_par_s = swap(a_par)
    dt = jnp.exp(ldt_ref[...])
    mag = jnp.exp(jnp.where(lo, a_par, a_par_s) * dt)
    ang = jnp.where(lo, a_par_s, a_par) * dt
    a1 = mag * jnp.where(lo, jnp.cos(ang), jnp.sin(ang))
    a1s = swap(a1)
    one = jnp.where(lo, 1.0, 0.0)
    sq = a_par * a_par
    den = sq + swap(sq)
    conj = jnp.where(lo, a_par, -a_par)
    re, im = product(a1 - one, swap(a1 - one), conj, swap(conj))
    f = jnp.where(lo, re, im) / den
    pw = [(jnp.broadcast_to(one, a1.shape), jnp.broadcast_to(swap(one), a1.shape))]
    for _ in range(CHUNK):
        re, im = product(pw[-1][0], pw[-1][1], a1, a1s)
        nxt = jnp.where(lo, re, im)
        pw.append((nxt, swap(nxt)))
    fs = swap(f)

    order = sorted(range(CHUNK), key=_token_slot)
    for g in range(gb):
        for d in range(2):
            r = 2 * g + d

            def row(v):
                return v[r:r + 1, :]

            c, b = cc_ref[g, d], bc_ref[g, d]
            cs, bs = swap(c), swap(b)
            re, im = product(b, bs, row(f), row(fs))
            bbar = jnp.where(lo, re, im)
            bbars = swap(bbar)
            bt_ref[g, d] = bbar
            for k in range(CHUNK + 1):
                re, im = product(c, cs, row(pw[k][0]), row(pw[k][1]))
                blk = jnp.where(lo, re, -im)
                if k < CHUNK:
                    kk = k if d == 0 else CHUNK - 1 - k
                    kc_ref[g, d, S5_GROUP * kk:S5_GROUP * (kk + 1), :] = blk
                t = k - 1 if d == 0 else CHUNK - k
                if 0 <= t < CHUNK:
                    slot = _token_slot(t)
                    rt_ref[g, S5_GROUP * slot:S5_GROUP * (slot + 1), half * d:half * (d + 1)] = blk.astype(BF16)
            for slot, s in enumerate(order):
                k = CHUNK - 1 - s if d == 0 else s
                re, im = product(bbar, bbars, row(pw[k][0]), row(pw[k][1]))
                rows_ = slice(S5_GROUP * slot, S5_GROUP * (slot + 1))
                st_ref[g, rows_, 2 * half * d:2 * half * d + half] = jnp.where(lo, re, im).astype(BF16)
                st_ref[g, rows_, 2 * half * d + half:2 * half * (d + 1)] = jnp.where(lo, im, -re).astype(BF16)
            p16, p16s = row(pw[CHUNK][0]), row(pw[CHUNK][1])
            coef_ref[g, 2 * d:2 * d + 1, :] = jnp.where(lo, p16, p16s)
            coef_ref[g, 2 * d + 1:2 * d + 2, :] = jnp.where(lo, -p16s, p16)


def _s5_tables(A_re, A_im, log_dt, B_re, B_im, C_re, C_im, d_skip):
    g = A_re.shape[1]
    gb = S5_BLOCK_GROUPS
    half = 2 * S5_STATE

    def rows_gd(x):
        return jnp.swapaxes(x, 0, 1)

    ac = rows_gd(jnp.concatenate([A_re, A_im], axis=-1)).reshape(2 * g, half)
    ldt = rows_gd(log_dt).reshape(2 * g, 1)
    cc = rows_gd(jnp.concatenate([C_re, C_im], axis=-1))
    bc = rows_gd(jnp.concatenate([B_re.transpose(0, 1, 3, 2), B_im.transpose(0, 1, 3, 2)], axis=-1))
    st, rt, kc, bt, coef = pl.pallas_call(
        _s5_tables_kernel,
        grid=(g // gb,),
        in_specs=[
            pl.BlockSpec((2 * gb, half), lambda i: (i, 0)),
            pl.BlockSpec((2 * gb, 1), lambda i: (i, 0)),
            pl.BlockSpec((gb, 2, S5_GROUP, half), lambda i: (i, 0, 0, 0)),
            pl.BlockSpec((gb, 2, S5_GROUP, half), lambda i: (i, 0, 0, 0)),
        ],
        out_specs=[
            pl.BlockSpec((gb, CHUNK_W, 4 * half), lambda i: (i, 0, 0)),
            pl.BlockSpec((gb, CHUNK_W, 2 * half), lambda i: (i, 0, 0)),
            pl.BlockSpec((gb, 2, CHUNK_W, half), lambda i: (i, 0, 0, 0)),
            pl.BlockSpec((gb, 2, S5_GROUP, half), lambda i: (i, 0, 0, 0)),
            pl.BlockSpec((gb, 4, half), lambda i: (i, 0, 0)),
        ],
        out_shape=[
            jax.ShapeDtypeStruct((g, CHUNK_W, 4 * half), BF16),
            jax.ShapeDtypeStruct((g, CHUNK_W, 2 * half), BF16),
            jax.ShapeDtypeStruct((g, 2, CHUNK_W, half), F32),
            jax.ShapeDtypeStruct((g, 2, S5_GROUP, half), F32),
            jax.ShapeDtypeStruct((g, 4, half), F32),
        ],
        compiler_params=_params(("arbitrary",)),
        name="s5_tables",
    )(ac, ldt, cc, bc)

    dtab = jnp.tile(d_skip.reshape(g, 1, S5_GROUP), (1, 1, CHUNK))
    col = jnp.arange(CHUNK_W)
    dst = jnp.array([_token_slot(t) for t in range(CHUNK)])[col // S5_GROUP] * S5_GROUP + col % S5_GROUP
    pm = (dst[:, None] == col[None, :]).astype(BF16)
    return st, rt, kc, bt, coef, dtab, pm


def _rope_tables(n_tokens):
    rows = n_tokens // GRID_W
    row = jnp.repeat(jnp.arange(rows), GRID_W).astype(F32)
    col = jnp.tile(jnp.arange(GRID_W), rows).astype(F32)
    n_freq = HEAD_DIM // 4
    inv_freq = ROPE_BASE ** (-jnp.arange(n_freq, dtype=F32) / n_freq)
    ar = row[:, None] * inv_freq
    ac = col[:, None] * inv_freq
    cos_t = jnp.concatenate([jnp.cos(ar), jnp.cos(ar), jnp.cos(ac), jnp.cos(ac)], axis=1)
    sin_t = jnp.concatenate([-jnp.sin(ar), jnp.sin(ar), -jnp.sin(ac), jnp.sin(ac)], axis=1)
    return cos_t, sin_t


def kernel(x, c, ctx, c_ctx, ada_w, ada_b, norm_pre, norm_post, attn_w_in, attn_w_out, attn_lam, attn_subln, s5_w_in, s5_A_re, s5_A_im, s5_log_dt, s5_B_re, s5_B_im, s5_C_re, s5_C_im, s5_D, s5_w_glu, s5_w_out):
    bsz, seq, d = x.shape
    n_ctx = ctx.shape[1]
    depth = ada_w.shape[0]
    assert d == D_MODEL and depth == 2 and bsz + 1 <= MOD_ROWS
    ctx_row = bsz

    cc = jnp.concatenate([c, c_ctx[None, :], jnp.zeros((MOD_ROWS - bsz - 1, d), F32)], axis=0)
    mods = _modulation(cc, ada_w, ada_b)
    mods = mods.reshape(depth, MOD_ROWS, 3, 1, d).transpose(0, 2, 1, 3, 4)

    x_lat = x.reshape(bsz * seq, d)
    x_ctx = ctx.reshape(bsz * n_ctx, d)
    norm_bm = 256
    lat_tiles = seq // norm_bm

    shift, scale, gate = mods[0, 0], mods[0, 1], mods[0, 2]
    h_lat = _normmod(x_lat, norm_pre[0], scale, shift, tiles_per_batch=lat_tiles, bm=norm_bm)
    h_ctx = _normmod(x_ctx, norm_pre[0], scale, shift, fixed_row=ctx_row, bm=norm_bm)

    w_in = attn_w_in[0].astype(BF16)
    cos_t, sin_t = _rope_tables(seq)
    qk_lat = _matmul_qk(h_lat, w_in, cos_t, sin_t, rope=True)
    vg_lat = _matmul(h_lat, w_in, col0=2 * QK_W, n=2 * QK_W)
    dummy = jnp.zeros((V7X_SUBLANES, HEAD_DIM), F32)
    qk_ctx = _matmul_qk(h_ctx, w_in, dummy, dummy, rope=False)
    vg_ctx = _matmul(h_ctx, w_in, col0=2 * QK_W, n=2 * QK_W)

    lam_init = 0.8 - 0.6 * math.exp(-0.3 * 0)
    a_lat = _attention(attn_lam[0], attn_subln[0], qk_lat, vg_lat,
                       [(qk_ctx, vg_ctx, n_ctx), (qk_lat, vg_lat, seq)],
                       batch=bsz, n_q=seq, tq=512, lam_init=lam_init)
    a_ctx = _attention(attn_lam[0], attn_subln[0], qk_ctx, vg_ctx,
                       [(qk_ctx, vg_ctx, n_ctx)],
                       batch=bsz, n_q=n_ctx, tq=n_ctx, lam_init=lam_init)
    w_out = attn_w_out[0].astype(BF16)
    o_lat = _matmul(a_lat, w_out)
    o_ctx = _matmul(a_ctx, w_out)
    gate0 = gate
    shift, scale, gate = mods[1, 0], mods[1, 1], mods[1, 2]
    x_lat, h_lat = _resid_normmod(x_lat, o_lat, norm_post[0], gate0, norm_pre[1], scale, shift,
                                  tiles_per_batch=lat_tiles, bm=norm_bm)
    _, h_ctx = _resid_normmod(x_ctx, o_ctx, norm_post[0], gate0, norm_pre[1], scale, shift,
                              fixed_row=ctx_row, bm=norm_bm)
    w_in = s5_w_in[0].astype(BF16)
    p_lat = _matmul(h_lat, w_in)
    u_ctx = _matmul(h_ctx, w_in, n=d)

    tables = _s5_tables(s5_A_re[0], s5_A_im[0], s5_log_dt[0], s5_B_re[0], s5_B_im[0],
                        s5_C_re[0], s5_C_im[0], s5_D[0])
    yact = _s5_mix(u_ctx, p_lat, *tables, batch=bsz, n_ctx=n_ctx, seq=seq)
    gl = _matmul_glu(yact, s5_w_glu[0].astype(BF16), p_lat)
    o_lat = _matmul(gl, s5_w_out[0].astype(BF16))
    x_lat = _resid(x_lat, o_lat, norm_post[1], gate, tiles_per_batch=lat_tiles, bm=norm_bm)
    return x_lat.reshape(bsz, seq, d)
```

```python
import functools
import math

import jax
import jax.numpy as jnp
from jax import lax
from jax.experimental import pallas as pl
from jax.experimental.pallas import tpu as pltpu

F32 = jnp.float32
BF16 = jnp.bfloat16

V7X_LANES = 128
V7X_SUBLANES = 8
V7X_VMEM_BYTES = 64 * 1024 * 1024
VMEM_LIMIT = V7X_VMEM_BYTES - 8 * 1024 * 1024

D_MODEL = 4096
GRID_W = 64
EPS = 1e-6
HEADS = 16
HEAD_DIM = 128
V_DIM = 2 * HEAD_DIM
QK_W = HEADS * 2 * HEAD_DIM
ROPE_BASE = 10000.0
S5_GROUP = 16
S5_STATE = 64
CHUNK = 16
CHUNK_W = CHUNK * S5_GROUP
MOD_ROWS = 8
_DN_LAST = (((1,), (1,)), ((), ()))


def _params(sem, vmem=VMEM_LIMIT):
    return pltpu.CompilerParams(dimension_semantics=sem, vmem_limit_bytes=vmem)


def _sigmoid(x):
    return 0.5 * jnp.tanh(0.5 * x) + 0.5


def _silu(x):
    return x * _sigmoid(x)


def _mod_kernel(c_ref, w_ref, b_ref, o_ref):
    a = _silu(c_ref[...]).astype(BF16)
    acc = jnp.dot(a, w_ref[0].astype(BF16), preferred_element_type=F32)
    o_ref[0] = acc + b_ref[0]


def _modulation(cc, ada_w, ada_b):
    depth, d, n = ada_w.shape
    bn = 1024
    return pl.pallas_call(
        _mod_kernel,
        grid=(depth, n // bn),
        in_specs=[
            pl.BlockSpec((MOD_ROWS, d), lambda i, j: (0, 0)),
            pl.BlockSpec((1, d, bn), lambda i, j: (i, 0, j)),
            pl.BlockSpec((1, 1, bn), lambda i, j: (i, 0, j)),
        ],
        out_specs=pl.BlockSpec((1, MOD_ROWS, bn), lambda i, j: (i, 0, j)),
        out_shape=jax.ShapeDtypeStruct((depth, MOD_ROWS, n), F32),
        compiler_params=_params(("arbitrary", "arbitrary")),
        name="modulation",
    )(cc, ada_w, ada_b.reshape(depth, 1, n))


def _rms(x):
    return x * lax.rsqrt(jnp.mean(x * x, axis=-1, keepdims=True) + EPS)


def _normmod_kernel(x_ref, g_ref, scale_ref, shift_ref, o_ref):
    y = _rms(x_ref[...]) * g_ref[...]
    o_ref[...] = (y * (1.0 + scale_ref[0]) + shift_ref[0]).astype(o_ref.dtype)


def _resid_kernel(x_ref, o_in_ref, g_ref, gate_ref, o_ref):
    y = _rms(o_in_ref[...].astype(F32)) * g_ref[...]
    o_ref[...] = x_ref[...] + gate_ref[0] * y


def _resid_normmod_kernel(x_ref, o_in_ref, g_post_ref, gate_ref, g_pre_ref, scale_ref, shift_ref, x_out_ref, h_ref):
    y = _rms(o_in_ref[...].astype(F32)) * g_post_ref[...]
    x = x_ref[...] + gate_ref[0] * y
    x_out_ref[...] = x
    h = _rms(x) * g_pre_ref[...]
    h_ref[...] = (h * (1.0 + scale_ref[0]) + shift_ref[0]).astype(h_ref.dtype)


def _mod_row_map(rows_per_mod_row, fixed_row):
    if fixed_row is not None:
        return lambda i: (fixed_row, 0, 0)
    return lambda i: (i // rows_per_mod_row, 0, 0)


def _normmod(x2, g, scale3, shift3, *, tiles_per_batch=None, fixed_row=None, bm=256):
    m, d = x2.shape
    row_map = _mod_row_map(tiles_per_batch, fixed_row)
    return pl.pallas_call(
        _normmod_kernel,
        grid=(m // bm,),
        in_specs=[
            pl.BlockSpec((bm, d), lambda i: (i, 0)),
            pl.BlockSpec((1, d), lambda i: (0, 0)),
            pl.BlockSpec((1, 1, d), row_map),
            pl.BlockSpec((1, 1, d), row_map),
        ],
        out_specs=pl.BlockSpec((bm, d), lambda i: (i, 0)),
        out_shape=jax.ShapeDtypeStruct((m, d), BF16),
        compiler_params=_params(("arbitrary",)),
        name="normmod",
    )(x2, g.reshape(1, d), scale3, shift3)


def _resid_normmod(x2, o2, g_post, gate3, g_pre, scale3, shift3, *, tiles_per_batch=None, fixed_row=None, bm=256):
    m, d = x2.shape
    row_map = _mod_row_map(tiles_per_batch, fixed_row)
    tile = pl.BlockSpec((bm, d), lambda i: (i, 0))
    vec = pl.BlockSpec((1, d), lambda i: (0, 0))
    mod = pl.BlockSpec((1, 1, d), row_map)
    return pl.pallas_call(
        _resid_normmod_kernel,
        grid=(m // bm,),
        in_specs=[tile, tile, vec, mod, vec, mod, mod],
        out_specs=[tile, tile],
        out_shape=[jax.ShapeDtypeStruct((m, d), F32), jax.ShapeDtypeStruct((m, d), BF16)],
        compiler_params=_params(("arbitrary",)),
        name="resid_normmod",
    )(x2, o2, g_post.reshape(1, d), gate3, g_pre.reshape(1, d), scale3, shift3)


def _resid(x2, o2, g, gate3, *, tiles_per_batch=None, fixed_row=None, bm=256):
    m, d = x2.shape
    row_map = _mod_row_map(tiles_per_batch, fixed_row)
    return pl.pallas_call(
        _resid_kernel,
        grid=(m // bm,),
        in_specs=[
            pl.BlockSpec((bm, d), lambda i: (i, 0)),
            pl.BlockSpec((bm, d), lambda i: (i, 0)),
            pl.BlockSpec((1, d), lambda i: (0, 0)),
            pl.BlockSpec((1, 1, d), row_map),
        ],
        out_specs=pl.BlockSpec((bm, d), lambda i: (i, 0)),
        out_shape=jax.ShapeDtypeStruct((m, d), F32),
        compiler_params=_params(("arbitrary",)),
        name="resid",
    )(x2, o2, g.reshape(1, d), gate3)


def _mm_kernel(x_ref, w_ref, o_ref):
    acc = jnp.dot(x_ref[...], w_ref[...], preferred_element_type=F32)
    o_ref[...] = acc.astype(o_ref.dtype)


def _mm_qk_kernel(x_ref, w_ref, cos_ref, sin_ref, o_ref, *, rope, q_tiles, q_scale):
    acc = jnp.dot(x_ref[...], w_ref[...], preferred_element_type=F32)
    sc = jnp.where(pl.program_id(1) < q_tiles, q_scale, 1.0).astype(F32)
    bn = acc.shape[1]
    if rope:
        cos = cos_ref[...]
        sin = sin_ref[...]
        lane = lax.broadcasted_iota(jnp.int32, (1, HEAD_DIM), 1)
        first = (lane % (HEAD_DIM // 2)) < (HEAD_DIM // 4)
    for c in range(bn // HEAD_DIM):
        xc = acc[:, c * HEAD_DIM:(c + 1) * HEAD_DIM]
        if rope:
            partner = jnp.where(first,
                                pltpu.roll(xc, HEAD_DIM - HEAD_DIM // 4, 1),
                                pltpu.roll(xc, HEAD_DIM // 4, 1))
            xc = xc * cos + partner * sin
        o_ref[:, c * HEAD_DIM:(c + 1) * HEAD_DIM] = (xc * sc).astype(o_ref.dtype)


def _mm_glu_kernel(x_ref, w_ref, y_ref, z_ref, o_ref):
    acc = jnp.dot(x_ref[...], w_ref[...], preferred_element_type=F32)
    y = y_ref[...].astype(F32)
    o_ref[...] = (y * _sigmoid(acc) * _silu(z_ref[...].astype(F32))).astype(o_ref.dtype)


def _mm_tiles(m, n):
    return min(m, 1024), min(n, 1024)


def _matmul(x, w, out_dtype=BF16, *, col0=0, n=None):
    m, k = x.shape
    n = w.shape[1] if n is None else n
    bm, bn = _mm_tiles(m, n)
    off = col0 // bn
    return pl.pallas_call(
        _mm_kernel,
        grid=(m // bm, n // bn),
        in_specs=[pl.BlockSpec((bm, k), lambda i, j: (i, 0)),
                  pl.BlockSpec((k, bn), lambda i, j: (0, off + j))],
        out_specs=pl.BlockSpec((bm, bn), lambda i, j: (i, j)),
        out_shape=jax.ShapeDtypeStruct((m, n), out_dtype),
        compiler_params=_params(("arbitrary", "arbitrary")),
        name="proj",
    )(x, w)


def _matmul_qk(x, w, cos_t, sin_t, *, rope):
    m, k = x.shape
    n = 2 * QK_W
    bm, bn = _mm_tiles(m, n)
    tiles_per_seq = cos_t.shape[0] // bm if rope else 1
    tab_map = (lambda i, j: (i % tiles_per_seq, 0)) if rope else (lambda i, j: (0, 0))
    tab_rows = bm if rope else V7X_SUBLANES
    kern = functools.partial(_mm_qk_kernel, rope=rope, q_tiles=(n // 2) // bn,
                             q_scale=HEAD_DIM ** -0.5 * math.log2(math.e))
    return pl.pallas_call(
        kern,
        grid=(m // bm, n // bn),
        in_specs=[pl.BlockSpec((bm, k), lambda i, j: (i, 0)),
                  pl.BlockSpec((k, bn), lambda i, j: (0, j)),
                  pl.BlockSpec((tab_rows, HEAD_DIM), tab_map),
                  pl.BlockSpec((tab_rows, HEAD_DIM), tab_map)],
        out_specs=pl.BlockSpec((bm, bn), lambda i, j: (i, j)),
        out_shape=jax.ShapeDtypeStruct((m, n), BF16),
        compiler_params=_params(("arbitrary", "arbitrary")),
        name="proj_qk_rope" if rope else "proj_qk",
    )(x, w, cos_t, sin_t)


def _matmul_glu(y, w, p):
    m, k = y.shape
    n = w.shape[1]
    bm, bn = _mm_tiles(m, n)
    z_off = n // bn
    return pl.pallas_call(
        _mm_glu_kernel,
        grid=(m // bm, n // bn),
        in_specs=[pl.BlockSpec((bm, k), lambda i, j: (i, 0)),
                  pl.BlockSpec((k, bn), lambda i, j: (0, j)),
                  pl.BlockSpec((bm, bn), lambda i, j: (i, j)),
                  pl.BlockSpec((bm, bn), lambda i, j: (i, z_off + j))],
        out_specs=pl.BlockSpec((bm, bn), lambda i, j: (i, j)),
        out_shape=jax.ShapeDtypeStruct((m, n), BF16),
        compiler_params=_params(("arbitrary", "arbitrary")),
        name="proj_glu",
    )(y, w, y, p)


ATT_KEY_CHUNK = 256
ATT_LAG = 4


def _lane_block_reduce(op, x):
    return functools.reduce(op, [x[:, V7X_LANES * j:V7X_LANES * (j + 1)] for j in range(x.shape[1] // V7X_LANES)])


def _attn_kernel(lam_ref, g_ref, q_ref, gate_ref, *refs, n_seg, lam_init):
    o_ref, s_scr = refs[2 * n_seg:]
    segs = [(refs[2 * i], refs[2 * i + 1]) for i in range(n_seg)]
    lv = lam_ref[...]
    t1 = jnp.sum(lv[0:1] * lv[1:2], axis=1, keepdims=True)
    t2 = jnp.sum(lv[2:3] * lv[3:4], axis=1, keepdims=True)
    lam = jnp.exp(t1) - jnp.exp(t2) + lam_init

    ch = s_scr.shape[3]
    chunks = []
    for si, (k, _) in enumerate(segs):
        for start in range(0, k.shape[0], ch):
            chunks.append((si, start, min(ch, k.shape[0] - start), len(chunks)))

    q = [q_ref[:, :HEAD_DIM], q_ref[:, HEAD_DIM:]]
    mrun = [None, None]
    for si, start, size, c in chunks:
        k_ref = segs[si][0]
        for n in range(2):
            s = lax.dot_general(q[n], k_ref[start:start + size, HEAD_DIM * n:HEAD_DIM * (n + 1)],
                                _DN_LAST, preferred_element_type=F32)
            s_scr[n, c, :, :size] = s
            part = _lane_block_reduce(jnp.maximum, s)
            mrun[n] = part if mrun[n] is None else jnp.maximum(mrun[n], part)
    m = [jnp.max(mrun[n], axis=1, keepdims=True) for n in range(2)]

    tq = q_ref.shape[0]
    m_bits = [pltpu.bitcast(jnp.broadcast_to(m[n], (tq, V7X_LANES)), jnp.uint32) for n in range(2)]
    lrun = [None, None]
    acc = [None, None]
    hist = [[], []]
    for idx, (si, start, size, c) in enumerate(chunks):
        v = segs[si][1][start:start + size, :]
        for n in range(2):
            mb = m_bits[n]
            if idx >= ATT_LAG:
                mb = (mb.reshape(tq // V7X_SUBLANES, V7X_SUBLANES, V7X_LANES) | hist[n][idx - ATT_LAG][None])
                mb = mb.reshape(tq, V7X_LANES)
            mc = pltpu.bitcast(mb, F32)
            p = jnp.concatenate([jnp.exp2(s_scr[n, c, :, V7X_LANES * j:V7X_LANES * (j + 1)] - mc)
                                 for j in range(size // V7X_LANES)], axis=1)
            part = _lane_block_reduce(jnp.add, p)
            lrun[n] = part if lrun[n] is None else lrun[n] + part
            pv = jnp.dot(p.astype(BF16), v, preferred_element_type=F32)
            acc[n] = pv if acc[n] is None else acc[n] + pv
            tail = pltpu.bitcast(pv[tq - V7X_SUBLANES:, V_DIM - V7X_LANES:], jnp.uint32)
            hist[n].append((tail >> 16) >> 16)
    l1 = jnp.sum(lrun[0], axis=1, keepdims=True)
    l2 = jnp.sum(lrun[1], axis=1, keepdims=True)
    o = acc[0] * (1.0 / l1) - acc[1] * (lam / l2)
    o = _rms(o) * g_ref[...] * (1.0 - lam_init)
    o_ref[...] = (o * _silu(gate_ref[...].astype(F32))).astype(o_ref.dtype)


def _attention(lam_vecs, subln_g, q_arr, gate_arr, segs, *, batch, n_q, tq, lam_init):
    nq_tiles = n_q // tq
    n_seg = len(segs)
    in_specs = [
        pl.BlockSpec((4, HEAD_DIM), lambda b, h, i: (0, 0)),
        pl.BlockSpec((1, V_DIM), lambda b, h, i: (0, 0)),
        pl.BlockSpec((tq, V_DIM), lambda b, h, i: (b * nq_tiles + i, h)),
        pl.BlockSpec((tq, V_DIM), lambda b, h, i: (b * nq_tiles + i, HEADS + h)),
    ]
    args = [lam_vecs, subln_g.reshape(1, V_DIM), q_arr, gate_arr]
    for k_arr, v_arr, rows in segs:
        in_specs.append(pl.BlockSpec((rows, V_DIM), lambda b, h, i: (b, HEADS + h)))
        in_specs.append(pl.BlockSpec((rows, V_DIM), lambda b, h, i: (b, h)))
        args += [k_arr, v_arr]
    kern = functools.partial(_attn_kernel, n_seg=n_seg, lam_init=lam_init)
    ch = min(ATT_KEY_CHUNK, max(rows for _, _, rows in segs))
    n_chunks = sum(pl.cdiv(rows, ch) for _, _, rows in segs)
    return pl.pallas_call(
        kern,
        grid=(batch, HEADS, nq_tiles),
        in_specs=in_specs,
        out_specs=pl.BlockSpec((tq, V_DIM), lambda b, h, i: (b * nq_tiles + i, h)),
        out_shape=jax.ShapeDtypeStruct((batch * n_q, HEADS * V_DIM), BF16),
        scratch_shapes=[pltpu.VMEM((2, n_chunks, tq, ch), F32)],
        compiler_params=_params(("arbitrary", "arbitrary", "arbitrary")),
        name="diff_attn",
    )(*args)


S5_BLOCK_GROUPS = V7X_LANES // S5_GROUP
S5_PASS_GROUPS = 4
S5_ROW_CHUNK = 32


def _shift_lanes_256(x, n, lane):
    a, b = x[:, :V7X_LANES], x[:, V7X_LANES:]
    zero = jnp.zeros_like(a)
    if n == 0:
        return x
    if n > 0:
        q, r = divmod(n, V7X_LANES)
        if r == 0:
            lo, hi = zero, a
        else:
            ra, rb = pltpu.roll(a, r, 1), pltpu.roll(b, r, 1)
            keep = lane >= r
            if q == 0:
                lo, hi = jnp.where(keep, ra, 0.0), jnp.where(keep, rb, ra)
            else:
                lo, hi = zero, jnp.where(keep, ra, 0.0)
    else:
        q, r = divmod(-n, V7X_LANES)
        if r == 0:
            lo, hi = b, zero
        else:
            ra, rb = pltpu.roll(a, V7X_LANES - r, 1), pltpu.roll(b, V7X_LANES - r, 1)
            keep = lane < V7X_LANES - r
            if q == 0:
                lo, hi = jnp.where(keep, ra, rb), jnp.where(keep, rb, 0.0)
            else:
                lo, hi = jnp.where(keep, rb, 0.0), zero
    return jnp.concatenate([lo, hi], axis=1)


def _token_slot(t):
    return (t % 2) * (CHUNK // 2) + t // 2


def _transpose_lane_blocks(vs, lane):
    vs = list(vs)
    d = len(vs) // 2
    while d >= 1:
        low = (lane // (S5_GROUP * d)) % 2 == 0
        for i in range(len(vs)):
            if i & d == 0:
                x, y = vs[i], vs[i + d]
                vs[i] = jnp.where(low, x, pltpu.roll(y, S5_GROUP * d, 1))
                vs[i + d] = jnp.where(low, pltpu.roll(x, V7X_LANES - S5_GROUP * d, 1), y)
        d //= 2
    return vs


def _s5_state_rows(batch, n_ctx_chunks, n_lat_chunks):
    ctx_pitch = n_ctx_chunks + V7X_SUBLANES
    lat_pitch = n_lat_chunks + V7X_SUBLANES
    lat_base = batch * ctx_pitch
    return ctx_pitch, lat_pitch, lat_base, lat_base + batch * lat_pitch


def _gelu_tanh(y):
    inner = math.sqrt(2.0 / math.pi) * (y + 0.044715 * (y * y * y))
    return 0.5 * y * (1.0 + jnp.tanh(inner))


def _s5_kernel(uctx_ref, ulat_ref, dtab_ref, st_ref, rt_ref, kc_ref, bt_ref, coef_ref, pm_ref, o_ref,
               w_scr, u_scr, s_scr, y_scr, m_scr, mt_scr, *, batch, n_ctx_chunks, n_lat_chunks):
    ng = S5_BLOCK_GROUPS
    rc = S5_ROW_CHUNK
    pairs = CHUNK // 2
    ctx_rows = batch * n_ctx_chunks
    lat_rows = batch * n_lat_chunks
    ctx_pitch, lat_pitch, lat_base = _s5_state_rows(batch, n_ctx_chunks, n_lat_chunks)[:3]
    lane = lax.broadcasted_iota(jnp.int32, (1, V7X_LANES), 1)

    w_scr[0:ctx_rows * pairs, :] = pltpu.bitcast(uctx_ref[...], jnp.uint32)
    w_scr[ctx_rows * pairs:, :] = pltpu.bitcast(ulat_ref[...], jnp.uint32)

    def gather(i, carry):
        r0 = pl.multiple_of(i * rc, rc)
        vs = [w_scr[pl.ds(r0 * pairs + k, rc, stride=pairs), :] for k in range(pairs)]
        for g, x in enumerate(_transpose_lane_blocks(vs, lane)):
            for par, bits in enumerate((x << 16, x & jnp.uint32(0xFFFF0000))):
                tok = lax.bitcast_convert_type(bits, F32)
                u_scr[g, pl.ds(r0, rc), V7X_LANES * par:V7X_LANES * (par + 1)] = tok.astype(BF16)
        return carry

    lax.fori_loop(0, (ctx_rows + lat_rows) // rc, gather, 0, unroll=2)

    for g in range(ng):
        kf = lax.dot_general(bt_ref[g, 0], kc_ref[g, 0], _DN_LAST, preferred_element_type=F32,
                             precision=lax.Precision.HIGHEST)
        kb = lax.dot_general(bt_ref[g, 1], kc_ref[g, 1], _DN_LAST, preferred_element_type=F32,
                             precision=lax.Precision.HIGHEST)
        for s in range(CHUNK):
            blk = (_shift_lanes_256(kf, S5_GROUP * s, lane)
                   + _shift_lanes_256(kb, -S5_GROUP * (CHUNK - 1 - s), lane))
            pos = _token_slot(s)
            mt_scr[S5_GROUP * pos:S5_GROUP * (pos + 1), :] = blk.astype(BF16)
        m_scr[g] = jnp.dot(mt_scr[...], pm_ref[...], preferred_element_type=F32).astype(BF16)

    for p in range(ng // S5_PASS_GROUPS):
        g0 = p * S5_PASS_GROUPS
        for gi in range(S5_PASS_GROUPS):
            o1 = jnp.dot(u_scr[g0 + gi], st_ref[g0 + gi], preferred_element_type=F32)
            for cb in range(4):
                col = o1[:, V7X_LANES * cb:V7X_LANES * (cb + 1)]
                for b in range(batch):
                    s_scr[gi, cb, ctx_pitch * b:ctx_pitch * b + n_ctx_chunks, :] = (
                        col[n_ctx_chunks * b:n_ctx_chunks * (b + 1)])
                    s_scr[gi, cb, lat_base + lat_pitch * b:lat_base + lat_pitch * b + n_lat_chunks, :] = (
                        col[ctx_rows + n_lat_chunks * b:ctx_rows + n_lat_chunks * (b + 1)])

        coefs = [[coef_ref[g0 + gi, i:i + 1, :] for i in range(4)] for gi in range(S5_PASS_GROUPS)]

        def make_step(base, n, pitch):
            def step(i, carry):
                out = []
                for gi in range(S5_PASS_GROUPS):
                    hf, hsf, hb, hsb = carry[4 * gi:4 * gi + 4]
                    rf = pl.ds(base + i, batch, stride=pitch)
                    rb = pl.ds(base + n - 1 - i, batch, stride=pitch)
                    sf, ssf = s_scr.at[gi, 0][rf, :], s_scr.at[gi, 1][rf, :]
                    sb, ssb = s_scr.at[gi, 2][rb, :], s_scr.at[gi, 3][rb, :]
                    s_scr.at[gi, 0][rf, :] = hf
                    s_scr.at[gi, 2][rb, :] = hb
                    caf, cbf, cab, cbb = coefs[gi]
                    out += [caf * hf + cbf * hsf + sf, caf * hsf - cbf * hf + ssf,
                            cab * hb + cbb * hsb + sb, cab * hsb - cbb * hb + ssb]
                return tuple(out)
            return step

        carry = tuple(jnp.zeros((batch, V7X_LANES), F32) for _ in range(4 * S5_PASS_GROUPS))
        carry = lax.fori_loop(0, n_ctx_chunks, make_step(0, n_ctx_chunks, ctx_pitch), carry)
        lax.fori_loop(0, n_lat_chunks, make_step(lat_base, n_lat_chunks, lat_pitch), carry)

        def lat_states(gi, cb):
            return jnp.concatenate([s_scr[gi, cb, lat_base + lat_pitch * b:lat_base + lat_pitch * b + n_lat_chunks, :]
                                    for b in range(batch)], axis=0)

        for gi in range(S5_PASS_GROUPS):
            g = g0 + gi
            hin = jnp.concatenate([lat_states(gi, 0), lat_states(gi, 2)], axis=1)
            y = jnp.dot(u_scr[g, ctx_rows:, :], m_scr[g], preferred_element_type=F32)
            y = y + lax.dot_general(hin.astype(BF16), rt_ref[g], _DN_LAST, preferred_element_type=F32)
            y = _gelu_tanh(y + dtab_ref[g] * u_scr[g, ctx_rows:, :].astype(F32))
            even = lax.bitcast_convert_type(y[:, :V7X_LANES].astype(BF16).astype(F32), jnp.uint32)
            odd = lax.bitcast_convert_type(y[:, V7X_LANES:].astype(BF16).astype(F32), jnp.uint32)
            y_scr[g] = (even >> 16) | odd

    def scatter(i, carry):
        r0 = pl.multiple_of(i * rc, rc)
        ys = [y_scr[g, pl.ds(r0, rc), :] for g in range(ng)]
        for k, x in enumerate(_transpose_lane_blocks(ys, lane)):
            w_scr[pl.ds((ctx_rows + r0) * pairs + k, rc, stride=pairs), :] = x
        return carry

    lax.fori_loop(0, lat_rows // rc, scatter, 0, unroll=2)
    o_ref[...] = pltpu.bitcast(w_scr[ctx_rows * pairs:, :], o_ref.dtype)


def _s5_mix(u_ctx, p_lat, st, rt, kc, bt, coef, dtab, pm, *, batch, n_ctx, seq):
    ng = S5_BLOCK_GROUPS
    width = u_ctx.shape[1]
    n_ctx_chunks, n_lat_chunks = n_ctx // CHUNK, seq // CHUNK
    rows = batch * (n_ctx_chunks + n_lat_chunks)
    lat_rows = batch * n_lat_chunks
    kern = functools.partial(_s5_kernel, batch=batch, n_ctx_chunks=n_ctx_chunks, n_lat_chunks=n_lat_chunks)
    return pl.pallas_call(
        kern,
        grid=(width // V7X_LANES,),
        in_specs=[
            pl.BlockSpec((batch * n_ctx, V7X_LANES), lambda i: (0, i)),
            pl.BlockSpec((batch * seq, V7X_LANES), lambda i: (0, i)),
            pl.BlockSpec((ng,) + dtab.shape[1:], lambda i: (i, 0, 0)),
            pl.BlockSpec((ng,) + st.shape[1:], lambda i: (i, 0, 0)),
            pl.BlockSpec((ng,) + rt.shape[1:], lambda i: (i, 0, 0)),
            pl.BlockSpec((ng,) + kc.shape[1:], lambda i: (i, 0, 0, 0)),
            pl.BlockSpec((ng,) + bt.shape[1:], lambda i: (i, 0, 0, 0)),
            pl.BlockSpec((ng,) + coef.shape[1:], lambda i: (i, 0, 0)),
            pl.BlockSpec(pm.shape, lambda i: (0, 0)),
        ],
        out_specs=pl.BlockSpec((batch * seq, V7X_LANES), lambda i: (0, i)),
        out_shape=jax.ShapeDtypeStruct((batch * seq, width), BF16),
        scratch_shapes=[
            pltpu.VMEM((rows * CHUNK // 2, V7X_LANES), jnp.uint32),
            pltpu.VMEM((ng, rows, CHUNK_W), BF16),
            pltpu.VMEM((S5_PASS_GROUPS, 4, _s5_state_rows(batch, n_ctx_chunks, n_lat_chunks)[3], V7X_LANES), F32),
            pltpu.VMEM((ng, lat_rows, V7X_LANES), jnp.uint32),
            pltpu.VMEM((ng, CHUNK_W, CHUNK_W), BF16),
            pltpu.VMEM((CHUNK_W, CHUNK_W), BF16),
        ],
        compiler_params=_params(("arbitrary",)),
        name="s5_mix",
    )(u_ctx, p_lat, dtab, st, rt, kc, bt, coef, pm)


def _s5_tables_kernel(ac_ref, ldt_ref, cc_ref, bc_ref, st_ref, rt_ref, kc_ref, bt_ref, coef_ref):
    gb = cc_ref.shape[0]
    half = 2 * S5_STATE
    lo = lax.broadcasted_iota(jnp.int32, (1, half), 1) < S5_STATE

    def swap(x):
        return pltpu.roll(x, S5_STATE, 1)

    def product(x, xs, y, ys):
        return x * y - xs * ys, x * ys + xs * y

    a_par = ac_ref[...]
    a_par_s = swap(a_par)
    dt = jnp.exp(ldt_ref[...])
    mag = jnp.exp(jnp.where(lo, a_par, a_par_s) * dt)
    ang = jnp.where(lo, a_par_s, a_par) * dt
    a1 = mag * jnp.where(lo, jnp.cos(ang), jnp.sin(ang))
    a1s = swap(a1)
    one = jnp.where(lo, 1.0, 0.0)
    sq = a_par * a_par
    den = sq + swap(sq)
    conj = jnp.where(lo, a_par, -a_par)
    re, im = product(a1 - one, swap(a1 - one), conj, swap(conj))
    f = jnp.where(lo, re, im) / den
    pw = [(jnp.broadcast_to(one, a1.shape), jnp.broadcast_to(swap(one), a1.shape))]
    for _ in range(CHUNK):
        re, im = product(pw[-1][0], pw[-1][1], a1, a1s)
        nxt = jnp.where(lo, re, im)
        pw.append((nxt, swap(nxt)))
    fs = swap(f)

    order = sorted(range(CHUNK), key=_token_slot)
    for g in range(gb):
        for d in range(2):
            r = 2 * g + d

            def row(v):
                return v[r:r + 1, :]

            c, b = cc_ref[g, d], bc_ref[g, d]
            cs, bs = swap(c), swap(b)
            re, im = product(b, bs, row(f), row(fs))
            bbar = jnp.where(lo, re, im)
            bbars = swap(bbar)
            bt_ref[g, d] = bbar
            for k in range(CHUNK + 1):
                re, im = product(c, cs, row(pw[k][0]), row(pw[k][1]))
                blk = jnp.where(lo, re, -im)
                if k < CHUNK:
                    kk = k if d == 0 else CHUNK - 1 - k
                    kc_ref[g, d, S5_GROUP * kk:S5_GROUP * (kk + 1), :] = blk
                t = k - 1 if d == 0 else CHUNK - k
                if 0 <= t < CHUNK:
                    slot = _token_slot(t)
                    rt_ref[g, S5_GROUP * slot:S5_GROUP * (slot + 1), half * d:half * (d + 1)] = blk.astype(BF16)
            for slot, s in enumerate(order):
                k = CHUNK - 1 - s if d == 0 else s
                re, im = product(bbar, bbars, row(pw[k][0]), row(pw[k][1]))
                rows_ = slice(S5_GROUP * slot, S5_GROUP * (slot + 1))
                st_ref[g, rows_, 2 * half * d:2 * half * d + half] = jnp.where(lo, re, im).astype(BF16)
                st_ref[g, rows_, 2 * half * d + half:2 * half * (d + 1)] = jnp.where(lo, im, -re).astype(BF16)
            p16, p16s = row(pw[CHUNK][0]), row(pw[CHUNK][1])
            coef_ref[g, 2 * d:2 * d + 1, :] = jnp.where(lo, p16, p16s)
            coef_ref[g, 2 * d + 1:2 * d + 2, :] = jnp.where(lo, -p16s, p16)


def _s5_tables(A_re, A_im, log_dt, B_re, B_im, C_re, C_im, d_skip):
    g = A_re.shape[1]
    gb = S5_BLOCK_GROUPS
    half = 2 * S5_STATE

    def rows_gd(x):
        return jnp.swapaxes(x, 0, 1)

    ac = rows_gd(jnp.concatenate([A_re, A_im], axis=-1)).reshape(2 * g, half)
    ldt = rows_gd(log_dt).reshape(2 * g, 1)
    cc = rows_gd(jnp.concatenate([C_re, C_im], axis=-1))
    bc = rows_gd(jnp.concatenate([B_re.transpose(0, 1, 3, 2), B_im.transpose(0, 1, 3, 2)], axis=-1))
    st, rt, kc, bt, coef = pl.pallas_call(
        _s5_tables_kernel,
        grid=(g // gb,),
        in_specs=[
            pl.BlockSpec((2 * gb, half), lambda i: (i, 0)),
            pl.BlockSpec((2 * gb, 1), lambda i: (i, 0)),
            pl.BlockSpec((gb, 2, S5_GROUP, half), lambda i: (i, 0, 0, 0)),
            pl.BlockSpec((gb, 2, S5_GROUP, half), lambda i: (i, 0, 0, 0)),
        ],
        out_specs=[
            pl.BlockSpec((gb, CHUNK_W, 4 * half), lambda i: (i, 0, 0)),
            pl.BlockSpec((gb, CHUNK_W, 2 * half), lambda i: (i, 0, 0)),
            pl.BlockSpec((gb, 2, CHUNK_W, half), lambda i: (i, 0, 0, 0)),
            pl.BlockSpec((gb, 2, S5_GROUP, half), lambda i: (i, 0, 0, 0)),
            pl.BlockSpec((gb, 4, half), lambda i: (i, 0, 0)),
        ],
        out_shape=[
            jax.ShapeDtypeStruct((g, CHUNK_W, 4 * half), BF16),
            jax.ShapeDtypeStruct((g, CHUNK_W, 2 * half), BF16),
            jax.ShapeDtypeStruct((g, 2, CHUNK_W, half), F32),
            jax.ShapeDtypeStruct((g, 2, S5_GROUP, half), F32),
            jax.ShapeDtypeStruct((g, 4, half), F32),
        ],
        compiler_params=_params(("arbitrary",)),
        name="s5_tables",
    )(ac, ldt, cc, bc)

    dtab = jnp.tile(d_skip.reshape(g, 1, S5_GROUP), (1, 1, CHUNK))
    col = jnp.arange(CHUNK_W)
    dst = jnp.array([_token_slot(t) for t in range(CHUNK)])[col // S5_GROUP] * S5_GROUP + col % S5_GROUP
    pm = (dst[:, None] == col[None, :]).astype(BF16)
    return st, rt, kc, bt, coef, dtab, pm


def _rope_tables(n_tokens):
    rows = n_tokens // GRID_W
    row = jnp.repeat(jnp.arange(rows), GRID_W).astype(F32)
    col = jnp.tile(jnp.arange(GRID_W), rows).astype(F32)
    n_freq = HEAD_DIM // 4
    inv_freq = ROPE_BASE ** (-jnp.arange(n_freq, dtype=F32) / n_freq)
    ar = row[:, None] * inv_freq
    ac = col[:, None] * inv_freq
    cos_t = jnp.concatenate([jnp.cos(ar), jnp.cos(ar), jnp.cos(ac), jnp.cos(ac)], axis=1)
    sin_t = jnp.concatenate([-jnp.sin(ar), jnp.sin(ar), -jnp.sin(ac), jnp.sin(ac)], axis=1)
    return cos_t, sin_t


def kernel(x, c, ctx, c_ctx, ada_w, ada_b, norm_pre, norm_post, attn_w_in, attn_w_out, attn_lam, attn_subln, s5_w_in, s5_A_re, s5_A_im, s5_log_dt, s5_B_re, s5_B_im, s5_C_re, s5_C_im, s5_D, s5_w_glu, s5_w_out):
    bsz, seq, d = x.shape
    n_ctx = ctx.shape[1]
    depth = ada_w.shape[0]
    assert d == D_MODEL and depth == 2 and bsz + 1 <= MOD_ROWS
    ctx_row = bsz

    cc = jnp.concatenate([c, c_ctx[None, :], jnp.zeros((MOD_ROWS - bsz - 1, d), F32)], axis=0)
    mods = _modulation(cc, ada_w, ada_b)
    mods = mods.reshape(depth, MOD_ROWS, 3, 1, d).transpose(0, 2, 1, 3, 4)

    x_lat = x.reshape(bsz * seq, d)
    x_ctx = ctx.reshape(bsz * n_ctx, d)
    norm_bm = 256
    lat_tiles = seq // norm_bm

    shift, scale, gate = mods[0, 0], mods[0, 1], mods[0, 2]
    h_lat = _normmod(x_lat, norm_pre[0], scale, shift, tiles_per_batch=lat_tiles, bm=norm_bm)
    h_ctx = _normmod(x_ctx, norm_pre[0], scale, shift, fixed_row=ctx_row, bm=norm_bm)

    w_in = attn_w_in[0].astype(BF16)
    cos_t, sin_t = _rope_tables(seq)
    qk_lat = _matmul_qk(h_lat, w_in, cos_t, sin_t, rope=True)
    vg_lat = _matmul(h_lat, w_in, col0=2 * QK_W, n=2 * QK_W)
    dummy = jnp.zeros((V7X_SUBLANES, HEAD_DIM), F32)
    qk_ctx = _matmul_qk(h_ctx, w_in, dummy, dummy, rope=False)
    vg_ctx = _matmul(h_ctx, w_in, col0=2 * QK_W, n=2 * QK_W)

    lam_init = 0.8 - 0.6 * math.exp(-0.3 * 0)
    a_lat = _attention(attn_lam[0], attn_subln[0], qk_lat, vg_lat,
                       [(qk_ctx, vg_ctx, n_ctx), (qk_lat, vg_lat, seq)],
                       batch=bsz, n_q=seq, tq=512, lam_init=lam_init)
    a_ctx = _attention(attn_lam[0], attn_subln[0], qk_ctx, vg_ctx,
                       [(qk_ctx, vg_ctx, n_ctx)],
                       batch=bsz, n_q=n_ctx, tq=n_ctx, lam_init=lam_init)
    w_out = attn_w_out[0].astype(BF16)
    o_lat = _matmul(a_lat, w_out)
    o_ctx = _matmul(a_ctx, w_out)
    gate0 = gate
    shift, scale, gate = mods[1, 0], mods[1, 1], mods[1, 2]
    x_lat, h_lat = _resid_normmod(x_lat, o_lat, norm_post[0], gate0, norm_pre[1], scale, shift,
                                  tiles_per_batch=lat_tiles, bm=norm_bm)
    _, h_ctx = _resid_normmod(x_ctx, o_ctx, norm_post[0], gate0, norm_pre[1], scale, shift,
                              fixed_row=ctx_row, bm=norm_bm)
    w_in = s5_w_in[0].astype(BF16)
    p_lat = _matmul(h_lat, w_in)
    u_ctx = _matmul(h_ctx, w_in, n=d)

    tables = _s5_tables(s5_A_re[0], s5_A_im[0], s5_log_dt[0], s5_B_re[0], s5_B_im[0],
                        s5_C_re[0], s5_C_im[0], s5_D[0])
    yact = _s5_mix(u_ctx, p_lat, *tables, batch=bsz, n_ctx=n_ctx, seq=seq)
    gl = _matmul_glu(yact, s5_w_glu[0].astype(BF16), p_lat)
    o_lat = _matmul(gl, s5_w_out[0].astype(BF16))
    x_lat = _resid(x_lat, o_lat, norm_post[1], gate, tiles_per_batch=lat_tiles, bm=norm_bm)
    return x_lat.reshape(bsz, seq, d)
```

```python
import functools
import math

import jax
import jax.numpy as jnp
from jax import lax
from jax.experimental import pallas as pl
from jax.experimental.pallas import tpu as pltpu

F32 = jnp.float32
BF16 = jnp.bfloat16

V7X_LANES = 128
V7X_SUBLANES = 8
V7X_VMEM_BYTES = 64 * 1024 * 1024
VMEM_LIMIT = V7X_VMEM_BYTES - 8 * 1024 * 1024

D_MODEL = 4096
GRID_W = 64
EPS = 1e-6
HEADS = 16
HEAD_DIM = 128
V_DIM = 2 * HEAD_DIM
QK_W = HEADS * 2 * HEAD_DIM
ROPE_BASE = 10000.0
S5_GROUP = 16
S5_STATE = 64
CHUNK = 16
CHUNK_W = CHUNK * S5_GROUP
MOD_ROWS = 8
_DN_LAST = (((1,), (1,)), ((), ()))


def _params(sem, vmem=VMEM_LIMIT):
    return pltpu.CompilerParams(dimension_semantics=sem, vmem_limit_bytes=vmem)


def _sigmoid(x):
    return 0.5 * jnp.tanh(0.5 * x) + 0.5


def _silu(x):
    return x * _sigmoid(x)


def _mod_kernel(c_ref, w_ref, b_ref, o_ref):
    a = _silu(c_ref[...]).astype(BF16)
    acc = jnp.dot(a, w_ref[0].astype(BF16), preferred_element_type=F32)
    o_ref[0] = acc + b_ref[0]


def _modulation(cc, ada_w, ada_b):
    depth, d, n = ada_w.shape
    bn = 1024
    return pl.pallas_call(
        _mod_kernel,
        grid=(depth, n // bn),
        in_specs=[
            pl.BlockSpec((MOD_ROWS, d), lambda i, j: (0, 0)),
            pl.BlockSpec((1, d, bn), lambda i, j: (i, 0, j)),
            pl.BlockSpec((1, 1, bn), lambda i, j: (i, 0, j)),
        ],
        out_specs=pl.BlockSpec((1, MOD_ROWS, bn), lambda i, j: (i, 0, j)),
        out_shape=jax.ShapeDtypeStruct((depth, MOD_ROWS, n), F32),
        compiler_params=_params(("arbitrary", "arbitrary")),
        name="modulation",
    )(cc, ada_w, ada_b.reshape(depth, 1, n))


def _rms(x):
    return x * lax.rsqrt(jnp.mean(x * x, axis=-1, keepdims=True) + EPS)


def _normmod_kernel(x_ref, g_ref, scale_ref, shift_ref, o_ref):
    y = _rms(x_ref[...]) * g_ref[...]
    o_ref[...] = (y * (1.0 + scale_ref[0]) + shift_ref[0]).astype(o_ref.dtype)


def _resid_kernel(x_ref, o_in_ref, g_ref, gate_ref, o_ref):
    y = _rms(o_in_ref[...].astype(F32)) * g_ref[...]
    o_ref[...] = x_ref[...] + gate_ref[0] * y


def _resid_normmod_kernel(x_ref, o_in_ref, g_post_ref, gate_ref, g_pre_ref, scale_ref, shift_ref, x_out_ref, h_ref):
    y = _rms(o_in_ref[...].astype(F32)) * g_post_ref[...]
    x = x_ref[...] + gate_ref[0] * y
    x_out_ref[...] = x
    h = _rms(x) * g_pre_ref[...]
    h_ref[...] = (h * (1.0 + scale_ref[0]) + shift_ref[0]).astype(h_ref.dtype)


def _mod_row_map(rows_per_mod_row, fixed_row):
    if fixed_row is not None:
        return lambda i: (fixed_row, 0, 0)
    return lambda i: (i // rows_per_mod_row, 0, 0)


def _normmod(x2, g, scale3, shift3, *, tiles_per_batch=None, fixed_row=None, bm=256):
    m, d = x2.shape
    row_map = _mod_row_map(tiles_per_batch, fixed_row)
    return pl.pallas_call(
        _normmod_kernel,
        grid=(m // bm,),
        in_specs=[
            pl.BlockSpec((bm, d), lambda i: (i, 0)),
            pl.BlockSpec((1, d), lambda i: (0, 0)),
            pl.BlockSpec((1, 1, d), row_map),
            pl.BlockSpec((1, 1, d), row_map),
        ],
        out_specs=pl.BlockSpec((bm, d), lambda i: (i, 0)),
        out_shape=jax.ShapeDtypeStruct((m, d), BF16),
        compiler_params=_params(("arbitrary",)),
        name="normmod",
    )(x2, g.reshape(1, d), scale3, shift3)


def _resid_normmod(x2, o2, g_post, gate3, g_pre, scale3, shift3, *, tiles_per_batch=None, fixed_row=None, bm=256):
    m, d = x2.shape
    row_map = _mod_row_map(tiles_per_batch, fixed_row)
    tile = pl.BlockSpec((bm, d), lambda i: (i, 0))
    vec = pl.BlockSpec((1, d), lambda i: (0, 0))
    mod = pl.BlockSpec((1, 1, d), row_map)
    return pl.pallas_call(
        _resid_normmod_kernel,
        grid=(m // bm,),
        in_specs=[tile, tile, vec, mod, vec, mod, mod],
        out_specs=[tile, tile],
        out_shape=[jax.ShapeDtypeStruct((m, d), F32), jax.ShapeDtypeStruct((m, d), BF16)],
        compiler_params=_params(("arbitrary",)),
        name="resid_normmod",
    )(x2, o2, g_post.reshape(1, d), gate3, g_pre.reshape(1, d), scale3, shift3)


def _resid(x2, o2, g, gate3, *, tiles_per_batch=None, fixed_row=None, bm=256):
    m, d = x2.shape
    row_map = _mod_row_map(tiles_per_batch, fixed_row)
    return pl.pallas_call(
        _resid_kernel,
        grid=(m // bm,),
        in_specs=[
            pl.BlockSpec((bm, d), lambda i: (i, 0)),
            pl.BlockSpec((bm, d), lambda i: (i, 0)),
            pl.BlockSpec((1, d), lambda i: (0, 0)),
            pl.BlockSpec((1, 1, d), row_map),
        ],
        out_specs=pl.BlockSpec((bm, d), lambda i: (i, 0)),
        out_shape=jax.ShapeDtypeStruct((m, d), F32),
        compiler_params=_params(("arbitrary",)),
        name="resid",
    )(x2, o2, g.reshape(1, d), gate3)


def _mm_kernel(x_ref, w_ref, o_ref):
    acc = jnp.dot(x_ref[...], w_ref[...], preferred_element_type=F32)
    o_ref[...] = acc.astype(o_ref.dtype)


def _mm_qk_kernel(x_ref, w_ref, cos_ref, sin_ref, o_ref, *, rope, q_tiles, q_scale):
    acc = jnp.dot(x_ref[...], w_ref[...], preferred_element_type=F32)
    sc = jnp.where(pl.program_id(1) < q_tiles, q_scale, 1.0).astype(F32)
    bn = acc.shape[1]
    if rope:
        cos = cos_ref[...]
        sin = sin_ref[...]
        lane = lax.broadcasted_iota(jnp.int32, (1, HEAD_DIM), 1)
        first = (lane % (HEAD_DIM // 2)) < (HEAD_DIM // 4)
    for c in range(bn // HEAD_DIM):
        xc = acc[:, c * HEAD_DIM:(c + 1) * HEAD_DIM]
        if rope:
            partner = jnp.where(first,
                                pltpu.roll(xc, HEAD_DIM - HEAD_DIM // 4, 1),
                                pltpu.roll(xc, HEAD_DIM // 4, 1))
            xc = xc * cos + partner * sin
        o_ref[:, c * HEAD_DIM:(c + 1) * HEAD_DIM] = (xc * sc).astype(o_ref.dtype)


def _mm_glu_kernel(x_ref, w_ref, y_ref, z_ref, o_ref):
    acc = jnp.dot(x_ref[...], w_ref[...], preferred_element_type=F32)
    y = y_ref[...].astype(F32)
    o_ref[...] = (y * _sigmoid(acc) * _silu(z_ref[...].astype(F32))).astype(o_ref.dtype)


def _mm_tiles(m, n):
    return min(m, 1024), min(n, 1024)


def _matmul(x, w, out_dtype=BF16, *, col0=0, n=None):
    m, k = x.shape
    n = w.shape[1] if n is None else n
    bm, bn = _mm_tiles(m, n)
    off = col0 // bn
    return pl.pallas_call(
        _mm_kernel,
        grid=(m // bm, n // bn),
        in_specs=[pl.BlockSpec((bm, k), lambda i, j: (i, 0)),
                  pl.BlockSpec((k, bn), lambda i, j: (0, off + j))],
        out_specs=pl.BlockSpec((bm, bn), lambda i, j: (i, j)),
        out_shape=jax.ShapeDtypeStruct((m, n), out_dtype),
        compiler_params=_params(("arbitrary", "arbitrary")),
        name="proj",
    )(x, w)


def _matmul_qk(x, w, cos_t, sin_t, *, rope):
    m, k = x.shape
    n = 2 * QK_W
    bm, bn = _mm_tiles(m, n)
    tiles_per_seq = cos_t.shape[0] // bm if rope else 1
    tab_map = (lambda i, j: (i % tiles_per_seq, 0)) if rope else (lambda i, j: (0, 0))
    tab_rows = bm if rope else V7X_SUBLANES
    kern = functools.partial(_mm_qk_kernel, rope=rope, q_tiles=(n // 2) // bn,
                             q_scale=HEAD_DIM ** -0.5 * math.log2(math.e))
    return pl.pallas_call(
        kern,
        grid=(m // bm, n // bn),
        in_specs=[pl.BlockSpec((bm, k), lambda i, j: (i, 0)),
                  pl.BlockSpec((k, bn), lambda i, j: (0, j)),
                  pl.BlockSpec((tab_rows, HEAD_DIM), tab_map),
                  pl.BlockSpec((tab_rows, HEAD_DIM), tab_map)],
        out_specs=pl.BlockSpec((bm, bn), lambda i, j: (i, j)),
        out_shape=jax.ShapeDtypeStruct((m, n), BF16),
        compiler_params=_params(("arbitrary", "arbitrary")),
        name="proj_qk_rope" if rope else "proj_qk",
    )(x, w, cos_t, sin_t)


def _matmul_glu(y, w, p):
    m, k = y.shape
    n = w.shape[1]
    bm, bn = _mm_tiles(m, n)
    z_off = n // bn
    return pl.pallas_call(
        _mm_glu_kernel,
        grid=(m // bm, n // bn),
        in_specs=[pl.BlockSpec((bm, k), lambda i, j: (i, 0)),
                  pl.BlockSpec((k, bn), lambda i, j: (0, j)),
                  pl.BlockSpec((bm, bn), lambda i, j: (i, j)),
                  pl.BlockSpec((bm, bn), lambda i, j: (i, z_off + j))],
        out_specs=pl.BlockSpec((bm, bn), lambda i, j: (i, j)),
        out_shape=jax.ShapeDtypeStruct((m, n), BF16),
        compiler_params=_params(("arbitrary", "arbitrary")),
        name="proj_glu",
    )(y, w, y, p)


ATT_KEY_CHUNK = 256
ATT_LAG = 4


def _lane_block_reduce(op, x):
    return functools.reduce(op, [x[:, V7X_LANES * j:V7X_LANES * (j + 1)] for j in range(x.shape[1] // V7X_LANES)])


def _attn_kernel(lam_ref, g_ref, q_ref, gate_ref, *refs, n_seg, lam_init):
    o_ref, s_scr = refs[2 * n_seg:]
    segs = [(refs[2 * i], refs[2 * i + 1]) for i in range(n_seg)]
    lv = lam_ref[...]
    t1 = jnp.sum(lv[0:1] * lv[1:2], axis=1, keepdims=True)
    t2 = jnp.sum(lv[2:3] * lv[3:4], axis=1, keepdims=True)
    lam = jnp.exp(t1) - jnp.exp(t2) + lam_init

    ch = s_scr.shape[3]
    chunks = []
    for si, (k, _) in enumerate(segs):
        for start in range(0, k.shape[0], ch):
            chunks.append((si, start, min(ch, k.shape[0] - start), len(chunks)))

    q = [q_ref[:, :HEAD_DIM], q_ref[:, HEAD_DIM:]]
    mrun = [None, None]
    for si, start, size, c in chunks:
        k_ref = segs[si][0]
        for n in range(2):
            s = lax.dot_general(q[n], k_ref[start:start + size, HEAD_DIM * n:HEAD_DIM * (n + 1)],
                                _DN_LAST, preferred_element_type=F32)
            s_scr[n, c, :, :size] = s
            part = _lane_block_reduce(jnp.maximum, s)
            mrun[n] = part if mrun[n] is None else jnp.maximum(mrun[n], part)
    m = [jnp.max(mrun[n], axis=1, keepdims=True) for n in range(2)]

    tq = q_ref.shape[0]
    m_bits = [pltpu.bitcast(jnp.broadcast_to(m[n], (tq, V7X_LANES)), jnp.uint32) for n in range(2)]
    m_bits = [mb | ((mb >> 16) >> 16) for mb in m_bits]
    lrun = [None, None]
    acc = [None, None]
    hist = [[], []]
    for idx, (si, start, size, c) in enumerate(chunks):
        v = segs[si][1][start:start + size, :]
        for n in range(2):
            mb = m_bits[n]
            if idx >= ATT_LAG:
                mb = (mb.reshape(tq // V7X_SUBLANES, V7X_SUBLANES, V7X_LANES) | hist[n][idx - ATT_LAG][None])
                mb = mb.reshape(tq, V7X_LANES)
            mc = pltpu.bitcast(mb, F32)
            p = jnp.concatenate([jnp.exp2(s_scr[n, c, :, V7X_LANES * j:V7X_LANES * (j + 1)] - mc)
                                 for j in range(size // V7X_LANES)], axis=1)
            part = _lane_block_reduce(jnp.add, p)
            lrun[n] = part if lrun[n] is None else lrun[n] + part
            pv = jnp.dot(p.astype(BF16), v, preferred_element_type=F32)
            acc[n] = pv if acc[n] is None else acc[n] + pv
            tail = pltpu.bitcast(pv[tq - V7X_SUBLANES:, V_DIM - V7X_LANES:], jnp.uint32)
            hist[n].append((tail >> 16) >> 16)
    l1 = jnp.sum(lrun[0], axis=1, keepdims=True)
    l2 = jnp.sum(lrun[1], axis=1, keepdims=True)
    o = acc[0] * (1.0 / l1) - acc[1] * (lam / l2)
    o = _rms(o) * g_ref[...] * (1.0 - lam_init)
    o_ref[...] = (o * _silu(gate_ref[...].astype(F32))).astype(o_ref.dtype)


def _attention(lam_vecs, subln_g, q_arr, gate_arr, segs, *, batch, n_q, tq, lam_init):
    nq_tiles = n_q // tq
    n_seg = len(segs)
    in_specs = [
        pl.BlockSpec((4, HEAD_DIM), lambda b, h, i: (0, 0)),
        pl.BlockSpec((1, V_DIM), lambda b, h, i: (0, 0)),
        pl.BlockSpec((tq, V_DIM), lambda b, h, i: (b * nq_tiles + i, h)),
        pl.BlockSpec((tq, V_DIM), lambda b, h, i: (b * nq_tiles + i, HEADS + h)),
    ]
    args = [lam_vecs, subln_g.reshape(1, V_DIM), q_arr, gate_arr]
    for k_arr, v_arr, rows in segs:
        in_specs.append(pl.BlockSpec((rows, V_DIM), lambda b, h, i: (b, HEADS + h)))
        in_specs.append(pl.BlockSpec((rows, V_DIM), lambda b, h, i: (b, h)))
        args += [k_arr, v_arr]
    kern = functools.partial(_attn_kernel, n_seg=n_seg, lam_init=lam_init)
    ch = min(ATT_KEY_CHUNK, max(rows for _, _, rows in segs))
    n_chunks = sum(pl.cdiv(rows, ch) for _, _, rows in segs)
    return pl.pallas_call(
        kern,
        grid=(batch, HEADS, nq_tiles),
        in_specs=in_specs,
        out_specs=pl.BlockSpec((tq, V_DIM), lambda b, h, i: (b * nq_tiles + i, h)),
        out_shape=jax.ShapeDtypeStruct((batch * n_q, HEADS * V_DIM), BF16),
        scratch_shapes=[pltpu.VMEM((2, n_chunks, tq, ch), F32)],
        compiler_params=_params(("arbitrary", "arbitrary", "arbitrary")),
        name="diff_attn",
    )(*args)


S5_BLOCK_GROUPS = V7X_LANES // S5_GROUP
S5_PASS_GROUPS = 4
S5_ROW_CHUNK = 32


def _shift_lanes_256(x, n, lane):
    a, b = x[:, :V7X_LANES], x[:, V7X_LANES:]
    zero = jnp.zeros_like(a)
    if n == 0:
        return x
    if n > 0:
        q, r = divmod(n, V7X_LANES)
        if r == 0:
            lo, hi = zero, a
        else:
            ra, rb = pltpu.roll(a, r, 1), pltpu.roll(b, r, 1)
            keep = lane >= r
            if q == 0:
                lo, hi = jnp.where(keep, ra, 0.0), jnp.where(keep, rb, ra)
            else:
                lo, hi = zero, jnp.where(keep, ra, 0.0)
    else:
        q, r = divmod(-n, V7X_LANES)
        if r == 0:
            lo, hi = b, zero
        else:
            ra, rb = pltpu.roll(a, V7X_LANES - r, 1), pltpu.roll(b, V7X_LANES - r, 1)
            keep = lane < V7X_LANES - r
            if q == 0:
                lo, hi = jnp.where(keep, ra, rb), jnp.where(keep, rb, 0.0)
            else:
                lo, hi = jnp.where(keep, rb, 0.0), zero
    return jnp.concatenate([lo, hi], axis=1)


def _token_slot(t):
    return t


def _transpose_lane_blocks(vs, lane):
    vs = list(vs)
    d = len(vs) // 2
    while d >= 1:
        low = (lane // (S5_GROUP * d)) % 2 == 0
        for i in range(len(vs)):
            if i & d == 0:
                x, y = vs[i], vs[i + d]
                vs[i] = jnp.where(low, x, pltpu.roll(y, S5_GROUP * d, 1))
                vs[i + d] = jnp.where(low, pltpu.roll(x, V7X_LANES - S5_GROUP * d, 1), y)
        d //= 2
    return vs


def _s5_state_rows(batch, n_ctx_chunks, n_lat_chunks):
    ctx_pitch = n_ctx_chunks + V7X_SUBLANES
    lat_pitch = n_lat_chunks + V7X_SUBLANES
    lat_base = batch * ctx_pitch
    return ctx_pitch, lat_pitch, lat_base, lat_base + batch * lat_pitch


def _gelu_tanh(y):
    inner = math.sqrt(2.0 / math.pi) * (y + 0.044715 * (y * y * y))
    return 0.5 * y * (1.0 + jnp.tanh(inner))


def _s5_kernel(uctx_ref, ulat_ref, dtab_ref, st_ref, rt_ref, kc_ref, bt_ref, coef_ref, pm_ref, o_ref,
               w_scr, u_scr, s_scr, y_scr, m_scr, mt_scr, *, batch, n_ctx_chunks, n_lat_chunks):
    ng = S5_BLOCK_GROUPS
    rc = S5_ROW_CHUNK
    ctx_rows = batch * n_ctx_chunks
    lat_rows = batch * n_lat_chunks
    ctx_pitch, lat_pitch, lat_base = _s5_state_rows(batch, n_ctx_chunks, n_lat_chunks)[:3]
    lane = lax.broadcasted_iota(jnp.int32, (1, V7X_LANES), 1)

    w_scr[0:ctx_rows * CHUNK, :] = uctx_ref[...].astype(F32)
    w_scr[ctx_rows * CHUNK:, :] = ulat_ref[...].astype(F32)

    def gather(i, carry):
        r0 = pl.multiple_of(i * rc, rc)
        for h in range(CHUNK // ng):
            vs = [w_scr[pl.ds(r0 * CHUNK + ng * h + k, rc, stride=CHUNK), :] for k in range(ng)]
            for g, x in enumerate(_transpose_lane_blocks(vs, lane)):
                u_scr[g, pl.ds(r0, rc), V7X_LANES * h:V7X_LANES * (h + 1)] = x.astype(BF16)
        return carry

    lax.fori_loop(0, (ctx_rows + lat_rows) // rc, gather, 0, unroll=2)

    for g in range(ng):
        kf = lax.dot_general(bt_ref[g, 0], kc_ref[g, 0], _DN_LAST, preferred_element_type=F32,
                             precision=lax.Precision.HIGHEST)
        kb = lax.dot_general(bt_ref[g, 1], kc_ref[g, 1], _DN_LAST, preferred_element_type=F32,
                             precision=lax.Precision.HIGHEST)
        for s in range(CHUNK):
            blk = (_shift_lanes_256(kf, S5_GROUP * s, lane)
                   + _shift_lanes_256(kb, -S5_GROUP * (CHUNK - 1 - s), lane))
            pos = _token_slot(s)
            mt_scr[S5_GROUP * pos:S5_GROUP * (pos + 1), :] = blk.astype(BF16)
        m_scr[g] = jnp.dot(mt_scr[...], pm_ref[...], preferred_element_type=F32).astype(BF16)

    for p in range(ng // S5_PASS_GROUPS):
        g0 = p * S5_PASS_GROUPS
        for gi in range(S5_PASS_GROUPS):
            o1 = jnp.dot(u_scr[g0 + gi], st_ref[g0 + gi], preferred_element_type=F32)
            for cb in range(4):
                col = o1[:, V7X_LANES * cb:V7X_LANES * (cb + 1)]
                for b in range(batch):
                    s_scr[gi, cb, ctx_pitch * b:ctx_pitch * b + n_ctx_chunks, :] = (
                        col[n_ctx_chunks * b:n_ctx_chunks * (b + 1)])
                    s_scr[gi, cb, lat_base + lat_pitch * b:lat_base + lat_pitch * b + n_lat_chunks, :] = (
                        col[ctx_rows + n_lat_chunks * b:ctx_rows + n_lat_chunks * (b + 1)])

        coefs = [[coef_ref[g0 + gi, i:i + 1, :] for i in range(4)] for gi in range(S5_PASS_GROUPS)]

        def make_step(base, n, pitch):
            def step(i, carry):
                out = []
                for gi in range(S5_PASS_GROUPS):
                    hf, hsf, hb, hsb = carry[4 * gi:4 * gi + 4]
                    rf = pl.ds(base + i, batch, stride=pitch)
                    rb = pl.ds(base + n - 1 - i, batch, stride=pitch)
                    sf, ssf = s_scr.at[gi, 0][rf, :], s_scr.at[gi, 1][rf, :]
                    sb, ssb = s_scr.at[gi, 2][rb, :], s_scr.at[gi, 3][rb, :]
                    s_scr.at[gi, 0][rf, :] = hf
                    s_scr.at[gi, 2][rb, :] = hb
                    caf, cbf, cab, cbb = coefs[gi]
                    out += [caf * hf + cbf * hsf + sf, caf * hsf - cbf * hf + ssf,
                            cab * hb + cbb * hsb + sb, cab * hsb - cbb * hb + ssb]
                return tuple(out)
            return step

        carry = tuple(jnp.zeros((batch, V7X_LANES), F32) for _ in range(4 * S5_PASS_GROUPS))
        carry = lax.fori_loop(0, n_ctx_chunks, make_step(0, n_ctx_chunks, ctx_pitch), carry)
        lax.fori_loop(0, n_lat_chunks, make_step(lat_base, n_lat_chunks, lat_pitch), carry)

        def lat_states(gi, cb):
            return jnp.concatenate([s_scr[gi, cb, lat_base + lat_pitch * b:lat_base + lat_pitch * b + n_lat_chunks, :]
                                    for b in range(batch)], axis=0)

        for gi in range(S5_PASS_GROUPS):
            g = g0 + gi
            hin = jnp.concatenate([lat_states(gi, 0), lat_states(gi, 2)], axis=1)
            y = jnp.dot(u_scr[g, ctx_rows:, :], m_scr[g], preferred_element_type=F32)
            y = y + lax.dot_general(hin.astype(BF16), rt_ref[g], _DN_LAST, preferred_element_type=F32)
            y_scr[g] = _gelu_tanh(y + dtab_ref[g] * u_scr[g, ctx_rows:, :].astype(F32))

    def scatter(i, carry):
        r0 = pl.multiple_of(i * rc, rc)
        for h in range(CHUNK // ng):
            ys = [y_scr[g, pl.ds(r0, rc), V7X_LANES * h:V7X_LANES * (h + 1)] for g in range(ng)]
            for k, x in enumerate(_transpose_lane_blocks(ys, lane)):
                w_scr[pl.ds((ctx_rows + r0) * CHUNK + ng * h + k, rc, stride=CHUNK), :] = x
        return carry

    lax.fori_loop(0, lat_rows // rc, scatter, 0, unroll=2)
    o_ref[...] = w_scr[ctx_rows * CHUNK:, :].astype(o_ref.dtype)


def _s5_mix(u_ctx, p_lat, st, rt, kc, bt, coef, dtab, pm, *, batch, n_ctx, seq):
    ng = S5_BLOCK_GROUPS
    width = u_ctx.shape[1]
    n_ctx_chunks, n_lat_chunks = n_ctx // CHUNK, seq // CHUNK
    rows = batch * (n_ctx_chunks + n_lat_chunks)
    lat_rows = batch * n_lat_chunks
    kern = functools.partial(_s5_kernel, batch=batch, n_ctx_chunks=n_ctx_chunks, n_lat_chunks=n_lat_chunks)
    return pl.pallas_call(
        kern,
        grid=(width // V7X_LANES,),
        in_specs=[
            pl.BlockSpec((batch * n_ctx, V7X_LANES), lambda i: (0, i)),
            pl.BlockSpec((batch * seq, V7X_LANES), lambda i: (0, i), pipeline_mode=pl.Buffered(1)),
            pl.BlockSpec((ng,) + dtab.shape[1:], lambda i: (i, 0, 0)),
            pl.BlockSpec((ng,) + st.shape[1:], lambda i: (i, 0, 0)),
            pl.BlockSpec((ng,) + rt.shape[1:], lambda i: (i, 0, 0)),
            pl.BlockSpec((ng,) + kc.shape[1:], lambda i: (i, 0, 0, 0)),
            pl.BlockSpec((ng,) + bt.shape[1:], lambda i: (i, 0, 0, 0)),
            pl.BlockSpec((ng,) + coef.shape[1:], lambda i: (i, 0, 0)),
            pl.BlockSpec(pm.shape, lambda i: (0, 0)),
        ],
        out_specs=pl.BlockSpec((batch * seq, V7X_LANES), lambda i: (0, i)),
        out_shape=jax.ShapeDtypeStruct((batch * seq, width), BF16),
        scratch_shapes=[
            pltpu.VMEM((rows * CHUNK, V7X_LANES), F32),
            pltpu.VMEM((ng, rows, CHUNK_W), BF16),
            pltpu.VMEM((S5_PASS_GROUPS, 4, _s5_state_rows(batch, n_ctx_chunks, n_lat_chunks)[3], V7X_LANES), F32),
            pltpu.VMEM((ng, lat_rows, CHUNK_W), F32),
            pltpu.VMEM((ng, CHUNK_W, CHUNK_W), BF16),
            pltpu.VMEM((CHUNK_W, CHUNK_W), BF16),
        ],
        compiler_params=_params(("arbitrary",), vmem=V7X_VMEM_BYTES - 5 * 1024 * 1024),
        name="s5_mix",
    )(u_ctx, p_lat, dtab, st, rt, kc, bt, coef, pm)


def _s5_tables_kernel(ac_ref, ldt_ref, cc_ref, bc_ref, st_ref, rt_ref, kc_ref, bt_ref, coef_ref):
    gb = cc_ref.shape[0]
    half = 2 * S5_STATE
    lo = lax.broadcasted_iota(jnp.int32, (1, half), 1) < S5_STATE

    def swap(x):
        return pltpu.roll(x, S5_STATE, 1)

    def product(x, xs, y, ys):
        return x * y - xs * ys, x * ys + xs * y

    a_par = ac_ref[...]
    a_par_s = swap(a_par)
    dt = jnp.exp(ldt_ref[...])
    mag = jnp.exp(jnp.where(lo, a_par, a_par_s) * dt)
    ang = jnp.where(lo, a_par_s, a_par) * dt
    a1 = mag * jnp.where(lo, jnp.cos(ang), jnp.sin(ang))
    a1s = swap(a1)
    one = jnp.where(lo, 1.0, 0.0)
    sq = a_par * a_par
    den = sq + swap(sq)
    conj = jnp.where(lo, a_par, -a_par)
    re, im = product(a1 - one, swap(a1 - one), conj, swap(conj))
    f = jnp.where(lo, re, im) / den
    pw = [(jnp.broadcast_to(one, a1.shape), jnp.broadcast_to(swap(one), a1.shape))]
    for _ in range(CHUNK):
        re, im = product(pw[-1][0], pw[-1][1], a1, a1s)
        nxt = jnp.where(lo, re, im)
        pw.append((nxt, swap(nxt)))
    fs = swap(f)

    order = sorted(range(CHUNK), key=_token_slot)
    for g in range(gb):
        for d in range(2):
            r = 2 * g + d

            def row(v):
                return v[r:r + 1, :]

            c, b = cc_ref[g, d], bc_ref[g, d]
            cs, bs = swap(c), swap(b)
            re, im = product(b, bs, row(f), row(fs))
            bbar = jnp.where(lo, re, im)
            bbars = swap(bbar)
            bt_ref[g, d] = bbar
            for k in range(CHUNK + 1):
                re, im = product(c, cs, row(pw[k][0]), row(pw[k][1]))
                blk = jnp.where(lo, re, -im)
                if k < CHUNK:
                    kk = k if d == 0 else CHUNK - 1 - k
                    kc_ref[g, d, S5_GROUP * kk:S5_GROUP * (kk + 1), :] = blk
                t = k - 1 if d == 0 else CHUNK - k
                if 0 <= t < CHUNK:
                    slot = _token_slot(t)
                    rt_ref[g, S5_GROUP * slot:S5_GROUP * (slot + 1), half * d:half * (d + 1)] = blk.astype(BF16)
            for slot, s in enumerate(order):
                k = CHUNK - 1 - s if d == 0 else s
                re, im = product(bbar, bbars, row(pw[k][0]), row(pw[k][1]))
                rows_ = slice(S5_GROUP * slot, S5_GROUP * (slot + 1))
                st_ref[g, rows_, 2 * half * d:2 * half * d + half] = jnp.where(lo, re, im).astype(BF16)
                st_ref[g, rows_, 2 * half * d + half:2 * half * (d + 1)] = jnp.where(lo, im, -re).astype(BF16)
            p16, p16s = row(pw[CHUNK][0]), row(pw[CHUNK][1])
            coef_ref[g, 2 * d:2 * d + 1, :] = jnp.where(lo, p16, p16s)
            coef_ref[g, 2 * d + 1:2 * d + 2, :] = jnp.where(lo, -p16s, p16)


def _s5_tables(A_re, A_im, log_dt, B_re, B_im, C_re, C_im, d_skip):
    g = A_re.shape[1]
    gb = S5_BLOCK_GROUPS
    half = 2 * S5_STATE

    def rows_gd(x):
        return jnp.swapaxes(x, 0, 1)

    ac = rows_gd(jnp.concatenate([A_re, A_im], axis=-1)).reshape(2 * g, half)
    ldt = rows_gd(log_dt).reshape(2 * g, 1)
    cc = rows_gd(jnp.concatenate([C_re, C_im], axis=-1))
    bc = rows_gd(jnp.concatenate([B_re.transpose(0, 1, 3, 2), B_im.transpose(0, 1, 3, 2)], axis=-1))
    st, rt, kc, bt, coef = pl.pallas_call(
        _s5_tables_kernel,
        grid=(g // gb,),
        in_specs=[
            pl.BlockSpec((2 * gb, half), lambda i: (i, 0)),
            pl.BlockSpec((2 * gb, 1), lambda i: (i, 0)),
            pl.BlockSpec((gb, 2, S5_GROUP, half), lambda i: (i, 0, 0, 0)),
            pl.BlockSpec((gb, 2, S5_GROUP, half), lambda i: (i, 0, 0, 0)),
        ],
        out_specs=[
            pl.BlockSpec((gb, CHUNK_W, 4 * half), lambda i: (i, 0, 0)),
            pl.BlockSpec((gb, CHUNK_W, 2 * half), lambda i: (i, 0, 0)),
            pl.BlockSpec((gb, 2, CHUNK_W, half), lambda i: (i, 0, 0, 0)),
            pl.BlockSpec((gb, 2, S5_GROUP, half), lambda i: (i, 0, 0, 0)),
            pl.BlockSpec((gb, 4, half), lambda i: (i, 0, 0)),
        ],
        out_shape=[
            jax.ShapeDtypeStruct((g, CHUNK_W, 4 * half), BF16),
            jax.ShapeDtypeStruct((g, CHUNK_W, 2 * half), BF16),
            jax.ShapeDtypeStruct((g, 2, CHUNK_W, half), F32),
            jax.ShapeDtypeStruct((g, 2, S5_GROUP, half), F32),
            jax.ShapeDtypeStruct((g, 4, half), F32),
        ],
        compiler_params=_params(("arbitrary",)),
        name="s5_tables",
    )(ac, ldt, cc, bc)

    dtab = jnp.tile(d_skip.reshape(g, 1, S5_GROUP), (1, 1, CHUNK))
    col = jnp.arange(CHUNK_W)
    dst = jnp.array([_token_slot(t) for t in range(CHUNK)])[col // S5_GROUP] * S5_GROUP + col % S5_GROUP
    pm = (dst[:, None] == col[None, :]).astype(BF16)
    return st, rt, kc, bt, coef, dtab, pm


def _rope_tables(n_tokens):
    rows = n_tokens // GRID_W
    row = jnp.repeat(jnp.arange(rows), GRID_W).astype(F32)
    col = jnp.tile(jnp.arange(GRID_W), rows).astype(F32)
    n_freq = HEAD_DIM // 4
    inv_freq = ROPE_BASE ** (-jnp.arange(n_freq, dtype=F32) / n_freq)
    ar = row[:, None] * inv_freq
    ac = col[:, None] * inv_freq
    cos_t = jnp.concatenate([jnp.cos(ar), jnp.cos(ar), jnp.cos(ac), jnp.cos(ac)], axis=1)
    sin_t = jnp.concatenate([-jnp.sin(ar), jnp.sin(ar), -jnp.sin(ac), jnp.sin(ac)], axis=1)
    return cos_t, sin_t


def kernel(x, c, ctx, c_ctx, ada_w, ada_b, norm_pre, norm_post, attn_w_in, attn_w_out, attn_lam, attn_subln, s5_w_in, s5_A_re, s5_A_im, s5_log_dt, s5_B_re, s5_B_im, s5_C_re, s5_C_im, s5_D, s5_w_glu, s5_w_out):
    bsz, seq, d = x.shape
    n_ctx = ctx.shape[1]
    depth = ada_w.shape[0]
    assert d == D_MODEL and depth == 2 and bsz + 1 <= MOD_ROWS
    ctx_row = bsz

    cc = jnp.concatenate([c, c_ctx[None, :], jnp.zeros((MOD_ROWS - bsz - 1, d), F32)], axis=0)
    mods = _modulation(cc, ada_w, ada_b)
    mods = mods.reshape(depth, MOD_ROWS, 3, 1, d).transpose(0, 2, 1, 3, 4)

    x_lat = x.reshape(bsz * seq, d)
    x_ctx = ctx.reshape(bsz * n_ctx, d)
    norm_bm = 256
    lat_tiles = seq // norm_bm

    shift, scale, gate = mods[0, 0], mods[0, 1], mods[0, 2]
    h_lat = _normmod(x_lat, norm_pre[0], scale, shift, tiles_per_batch=lat_tiles, bm=norm_bm)
    h_ctx = _normmod(x_ctx, norm_pre[0], scale, shift, fixed_row=ctx_row, bm=norm_bm)

    w_in = attn_w_in[0].astype(BF16)
    cos_t, sin_t = _rope_tables(seq)
    qk_lat = _matmul_qk(h_lat, w_in, cos_t, sin_t, rope=True)
    vg_lat = _matmul(h_lat, w_in, col0=2 * QK_W, n=2 * QK_W)
    dummy = jnp.zeros((V7X_SUBLANES, HEAD_DIM), F32)
    qk_ctx = _matmul_qk(h_ctx, w_in, dummy, dummy, rope=False)
    vg_ctx = _matmul(h_ctx, w_in, col0=2 * QK_W, n=2 * QK_W)

    lam_init = 0.8 - 0.6 * math.exp(-0.3 * 0)
    a_lat = _attention(attn_lam[0], attn_subln[0], qk_lat, vg_lat,
                       [(qk_ctx, vg_ctx, n_ctx), (qk_lat, vg_lat, seq)],
                       batch=bsz, n_q=seq, tq=512, lam_init=lam_init)
    a_ctx = _attention(attn_lam[0], attn_subln[0], qk_ctx, vg_ctx,
                       [(qk_ctx, vg_ctx, n_ctx)],
                       batch=bsz, n_q=n_ctx, tq=n_ctx, lam_init=lam_init)
    w_out = attn_w_out[0].astype(BF16)
    o_lat = _matmul(a_lat, w_out)
    o_ctx = _matmul(a_ctx, w_out)
    gate0 = gate
    shift, scale, gate = mods[1, 0], mods[1, 1], mods[1, 2]
    x_lat, h_lat = _resid_normmod(x_lat, o_lat, norm_post[0], gate0, norm_pre[1], scale, shift,
                                  tiles_per_batch=lat_tiles, bm=norm_bm)
    _, h_ctx = _resid_normmod(x_ctx, o_ctx, norm_post[0], gate0, norm_pre[1], scale, shift,
                              fixed_row=ctx_row, bm=norm_bm)
    w_in = s5_w_in[0].astype(BF16)
    p_lat = _matmul(h_lat, w_in)
    u_ctx = _matmul(h_ctx, w_in, n=d)

    tables = _s5_tables(s5_A_re[0], s5_A_im[0], s5_log_dt[0], s5_B_re[0], s5_B_im[0],
                        s5_C_re[0], s5_C_im[0], s5_D[0])
    yact = _s5_mix(u_ctx, p_lat, *tables, batch=bsz, n_ctx=n_ctx, seq=seq)
    gl = _matmul_glu(yact, s5_w_glu[0].astype(BF16), p_lat)
    o_lat = _matmul(gl, s5_w_out[0].astype(BF16))
    x_lat = _resid(x_lat, o_lat, norm_post[1], gate, tiles_per_batch=lat_tiles, bm=norm_bm)
    return x_lat.reshape(bsz, seq, d)
```
